```python
import jax, jax.numpy as jnp
from jax import lax
import numpy as np

D_MODEL = 1024
BATCH = 8
SEQ = 2048
DEPTH = 2

GRID_W = 64
CTX_LEN = 256

HEAD_DIM = 64
N_Q_HEADS = 8
N_KV_HEADS = 2
GROUP = N_Q_HEADS // N_KV_HEADS
ATTN_W = N_Q_HEADS * HEAD_DIM
KV_W = N_KV_HEADS * HEAD_DIM
CONF_W = D_MODEL // 4
SC_W = D_MODEL // 4
MIX_W = ATTN_W + CONF_W + SC_W
IN_W = ATTN_W + 2 * KV_W + 2 * CONF_W + 3 * SC_W
SPLITS = tuple(int(s) for s in np.cumsum([ATTN_W, KV_W, KV_W, CONF_W, CONF_W, SC_W, SC_W]))

WINDOW = 128
BLOCK = 128
CONF_K = 31
SC_K = 3
FFN_K = 3
D_FF = 2816
ROPE_THETA = 10000.0
ROPE_FREQS = HEAD_DIM // 4
EPS = 1e-6
NEG = -1e30

kernel_name = "hybrid_parallel_group_dit_block"


def rms_norm(x, g):
    xf = x.astype(jnp.float32)
    y = xf * lax.rsqrt(jnp.mean(xf * xf, axis=-1, keepdims=True) + EPS)
    return (y * g.astype(jnp.float32)).astype(x.dtype)


def layer_norm(x, g, b):
    xf = x.astype(jnp.float32)
    mu = jnp.mean(xf, axis=-1, keepdims=True)
    var = jnp.mean(jnp.square(xf - mu), axis=-1, keepdims=True)
    y = (xf - mu) * lax.rsqrt(var + EPS)
    return (y * g.astype(jnp.float32) + b.astype(jnp.float32)).astype(x.dtype)


def dwconv(x, w):
    k, ch = w.shape
    pad = k // 2
    return lax.conv_general_dilated(
        x, w[:, None, :].astype(x.dtype), window_strides=(1,), padding=[(pad, pad)],
        dimension_numbers=("NWC", "WIO", "NWC"), feature_group_count=ch)


def axial_rope_tables(n_tokens):
    rows = n_tokens // GRID_W
    r, col = jnp.meshgrid(jnp.arange(rows), jnp.arange(GRID_W), indexing="ij")
    pos = jnp.stack([r.reshape(-1), col.reshape(-1)], axis=-1).astype(jnp.float32)
    inv = ROPE_THETA ** (-jnp.arange(ROPE_FREQS, dtype=jnp.float32) / ROPE_FREQS)
    ang = pos[:, :, None] * inv
    return jnp.cos(ang), jnp.sin(ang)


def apply_rope(x, cos, sin):
    b, l, h, _ = x.shape
    xr = x.astype(jnp.float32).reshape(b, l, h, 2, 2, ROPE_FREQS)
    a, bb = xr[..., 0, :], xr[..., 1, :]
    cs, sn = cos[None, :, None], sin[None, :, None]
    out = jnp.stack([a * cs - bb * sn, bb * cs + a * sn], axis=-2)
    return out.reshape(b, l, h, HEAD_DIM).astype(x.dtype)


def heads(t, n):
    return t.reshape(t.shape[0], t.shape[1], n, HEAD_DIM)


def sink_logits(sink, shape_prefix, n_q):
    s = sink.astype(jnp.float32).reshape(N_KV_HEADS, GROUP)[:, :, None, None]
    return jnp.broadcast_to(s, shape_prefix + (N_KV_HEADS, GROUP, n_q, 1))


def latent_window_attention(q, k, v, kc, vc, sink):
    b, l = q.shape[:2]
    nblk = l // BLOCK
    scale = HEAD_DIM ** -0.5
    qb = q.reshape(b, nblk, BLOCK, N_KV_HEADS, GROUP, HEAD_DIM)
    pad = ((0, 0), (BLOCK, BLOCK), (0, 0), (0, 0))
    kp = jnp.pad(k, pad).reshape(b, nblk + 2, BLOCK, N_KV_HEADS, HEAD_DIM)
    vp = jnp.pad(v, pad).reshape(b, nblk + 2, BLOCK, N_KV_HEADS, HEAD_DIM)
    kband = jnp.concatenate([kp[:, :-2], kp[:, 1:-1], kp[:, 2:]], axis=2)
    vband = jnp.concatenate([vp[:, :-2], vp[:, 1:-1], vp[:, 2:]], axis=2)
    s_loc = jnp.einsum("bnqhgd,bnkhd->bnhgqk", qb, kband,
                       preferred_element_type=jnp.float32) * scale
    blk = jnp.arange(nblk)[:, None, None] * BLOCK
    qpos = blk + jnp.arange(BLOCK)[None, :, None]
    kpos = blk - BLOCK + jnp.arange(3 * BLOCK)[None, None, :]
    mask = (jnp.abs(kpos - qpos) <= WINDOW) & (kpos >= 0) & (kpos < l)
    s_loc = jnp.where(mask[None, :, None, None], s_loc, NEG)
    s_ctx = jnp.einsum("bnqhgd,bchd->bnhgqc", qb, kc,
                       preferred_element_type=jnp.float32) * scale
    s_snk = sink_logits(sink, (b, nblk), BLOCK)
    p = jax.nn.softmax(jnp.concatenate([s_loc, s_ctx, s_snk], axis=-1), axis=-1)
    n_loc = 3 * BLOCK
    n_ctx = kc.shape[1]
    p_loc = p[..., :n_loc].astype(v.dtype)
    p_ctx = p[..., n_loc:n_loc + n_ctx].astype(v.dtype)
    o = (jnp.einsum("bnhgqk,bnkhd->bnqhgd", p_loc, vband)
         + jnp.einsum("bnhgqc,bchd->bnqhgd", p_ctx, vc))
    return o.reshape(b, l, ATTN_W)


def context_attention(qc, kc, vc, sink):
    b, lc = qc.shape[:2]
    scale = HEAD_DIM ** -0.5
    qg = qc.reshape(b, lc, N_KV_HEADS, GROUP, HEAD_DIM)
    s = jnp.einsum("bqhgd,bchd->bhgqc", qg, kc, preferred_element_type=jnp.float32) * scale
    s_snk = sink_logits(sink, (b,), lc)
    p = jax.nn.softmax(jnp.concatenate([s, s_snk], axis=-1), axis=-1)
    o = jnp.einsum("bhgqc,bchd->bqhgd", p[..., :lc].astype(vc.dtype), vc)
    return o.reshape(b, lc, ATTN_W)


def conformer_conv(val, gate, w_dw, b_dw, ln_g, ln_b):
    u = val * jax.nn.sigmoid(gate)
    u = dwconv(u, w_dw) + b_dw
    return jax.nn.silu(layer_norm(u, ln_g, ln_b))


def short_conv(bg, cg, u, w_dw):
    return bg * dwconv(cg * u, w_dw)


def merge_groups(a, cf, s, g_group, w_out):
    ga, gc, gs = jnp.split(g_group, [ATTN_W, ATTN_W + CONF_W])
    z = jnp.concatenate([rms_norm(a, ga), rms_norm(cf, gc), rms_norm(s, gs)], axis=-1)
    return z @ w_out


def token_mixer(h, hc, w_in, sink, w_conf_dw, b_conf_dw, conf_ln_g, conf_ln_b,
                w_sc_dw, g_group, w_out, cos, sin, update_ctx):
    q, k, v, cv, cgt, sb, scg, su = jnp.split(h @ w_in, SPLITS, axis=-1)
    if update_ctx:
        qc, kc, vc, cvc, cgc, sbc, scc, suc = jnp.split(hc @ w_in, SPLITS, axis=-1)
    else:
        kc, vc = jnp.split(hc @ w_in[:, ATTN_W:ATTN_W + 2 * KV_W], 2, axis=-1)
    kc, vc = heads(kc, N_KV_HEADS), heads(vc, N_KV_HEADS)
    q = apply_rope(heads(q, N_Q_HEADS), cos, sin)
    k = apply_rope(heads(k, N_KV_HEADS), cos, sin)
    a = latent_window_attention(q, k, heads(v, N_KV_HEADS), kc, vc, sink)
    cf = conformer_conv(cv, cgt, w_conf_dw, b_conf_dw, conf_ln_g, conf_ln_b)
    s = short_conv(sb, scg, su, w_sc_dw)
    y = merge_groups(a, cf, s, g_group, w_out)
    if not update_ctx:
        return y, None
    ac = context_attention(heads(qc, N_Q_HEADS), kc, vc, sink)
    cfc = conformer_conv(cvc, cgc, w_conf_dw, b_conf_dw, conf_ln_g, conf_ln_b)
    sc_ = short_conv(sbc, scc, suc, w_sc_dw)
    yc = merge_groups(ac, cfc, sc_, g_group, w_out)
    return y, yc


def conv_ffn(h, w_up, w_dw, w_down):
    u = dwconv(h @ w_up, w_dw)
    gate, val = jnp.split(u, 2, axis=-1)
    return (jax.nn.silu(gate) * val) @ w_down


def setup_inputs(seed: int = 0) -> dict:
    key = jax.random.key(seed)
    ks = jax.random.split(key, 24)
    f32 = jnp.float32
    nrm = lambda k, shape, s: jax.random.normal(k, shape, f32) * s
    gain = lambda k, shape: 1.0 + 0.02 * jax.random.normal(k, shape, f32)
    L = DEPTH
    return {
        "x": nrm(ks[0], (BATCH, SEQ, D_MODEL), 1.0),
        "c": nrm(ks[1], (BATCH, D_MODEL), 1.0),
        "ctx": nrm(ks[2], (BATCH, CTX_LEN, D_MODEL), 1.0),
        "c_ctx": nrm(ks[3], (D_MODEL,), 1.0),
        "w_ada": nrm(ks[4], (L, D_MODEL, 6 * D_MODEL), 0.5 * D_MODEL ** -0.5),
        "b_ada": nrm(ks[5], (L, 6 * D_MODEL), 0.02),
        "g_pre_mix": gain(ks[6], (L, D_MODEL)),
        "g_post_mix": gain(ks[7], (L, D_MODEL)),
        "g_pre_ffn": gain(ks[8], (L, D_MODEL)),
        "g_post_ffn": gain(ks[9], (L, D_MODEL)),
        "w_in": nrm(ks[10], (L, D_MODEL, IN_W), D_MODEL ** -0.5),
        "sink": nrm(ks[11], (L, N_Q_HEADS), 0.5),
        "w_conf_dw": nrm(ks[12], (L, CONF_K, CONF_W), CONF_K ** -0.5),
        "b_conf_dw": nrm(ks[13], (L, CONF_W), 0.02),
        "conf_ln_g": gain(ks[14], (L, CONF_W)),
        "conf_ln_b": nrm(ks[15], (L, CONF_W), 0.02),
        "w_sc_dw": nrm(ks[16], (L, SC_K, SC_W), SC_K ** -0.5),
        "g_group": gain(ks[17], (L, MIX_W)),
        "w_out": nrm(ks[18], (L, MIX_W, D_MODEL), MIX_W ** -0.5),
        "w_up": nrm(ks[19], (L, D_MODEL, 2 * D_FF), D_MODEL ** -0.5),
        "w_ffn_dw": nrm(ks[20], (L, FFN_K, 2 * D_FF), FFN_K ** -0.5),
        "w_down": nrm(ks[21], (L, D_FF, D_MODEL), D_FF ** -0.5),
    }


def reference(x, c, ctx, c_ctx, w_ada, b_ada, g_pre_mix, g_post_mix, g_pre_ffn, g_post_ffn,
              w_in, sink, w_conf_dw, b_conf_dw, conf_ln_g, conf_ln_b, w_sc_dw, g_group,
              w_out, w_up, w_ffn_dw, w_down):
    n_tok = x.shape[1]
    cos, sin = axial_rope_tables(n_tok)
    for l in range(DEPTH):
        update_ctx = l < DEPTH - 1
        mod = jax.nn.silu(c) @ w_ada[l] + b_ada[l]
        sh1, sc1, gt1, sh2, sc2, gt2 = [m[:, None, :] for m in jnp.split(mod, 6, axis=-1)]
        modc = jax.nn.silu(c_ctx) @ w_ada[l] + b_ada[l]
        sh1c, sc1c, gt1c, sh2c, sc2c, gt2c = jnp.split(modc, 6, axis=-1)
        h = rms_norm(x, g_pre_mix[l]) * (1.0 + sc1) + sh1
        hc = rms_norm(ctx, g_pre_mix[l]) * (1.0 + sc1c) + sh1c
        y, yc = token_mixer(h, hc, w_in[l], sink[l], w_conf_dw[l], b_conf_dw[l], conf_ln_g[l],
                            conf_ln_b[l], w_sc_dw[l], g_group[l], w_out[l], cos, sin, update_ctx)
        x = x + gt1 * rms_norm(y, g_post_mix[l])
        h = rms_norm(x, g_pre_ffn[l]) * (1.0 + sc2) + sh2
        x = x + gt2 * rms_norm(conv_ffn(h, w_up[l], w_ffn_dw[l], w_down[l]), g_post_ffn[l])
        if update_ctx:
            ctx = ctx + gt1c * rms_norm(yc, g_post_mix[l])
            hc = rms_norm(ctx, g_pre_ffn[l]) * (1.0 + sc2c) + sh2c
            ctx = ctx + gt2c * rms_norm(conv_ffn(hc, w_up[l], w_ffn_dw[l], w_down[l]), g_post_ffn[l])
    return x
```

```python
import functools

import numpy as np
import jax
import jax.numpy as jnp
from jax import lax
from jax.experimental import pallas as pl
from jax.experimental.pallas import tpu as pltpu

F32 = jnp.float32
BF16 = jnp.bfloat16

V7X_LANES = 128
V7X_BF16_SUBLANES = 16
V7X_VMEM_LIMIT_BYTES = 56 * 1024 * 1024

HEAD_DIM = 64
N_Q_HEADS = 8
N_KV_HEADS = 2
GROUP = N_Q_HEADS // N_KV_HEADS
GRID_W = 64
WINDOW = 128
ATTN_BLOCK = 128
ROPE_FREQS = HEAD_DIM // 4
ROPE_THETA = 10000.0
EPS = 1e-6
NEG = -1e30

ATTN_W = N_Q_HEADS * HEAD_DIM
KV_W = N_KV_HEADS * HEAD_DIM
KV_DUP_W = 2 * KV_W
N_MOD = 6
MOD_ROWS = 16

IN_TM = 512
MIX_TQ = 256
FFN_TM = 512
FFN_TF = 256
FFN_HALO = V7X_BF16_SUBLANES
CONV_RC = 64
CONV_PAD = 16
MOD_TN = 1536


def _rms(x, eps=EPS):
    return x * lax.rsqrt(jnp.mean(x * x, axis=-1, keepdims=True) + eps)


def _dot(a, b):
    return jnp.dot(a, b, preferred_element_type=F32)


def _dot_nt(a, b):
    return lax.dot_general(a, b, (((1,), (1,)), ((), ())), preferred_element_type=F32)


def _params(*sem):
    return pltpu.CompilerParams(dimension_semantics=sem, vmem_limit_bytes=V7X_VMEM_LIMIT_BYTES)


def _mod_kernel(c_ref, w_ref, b_ref, o_ref):
    c = c_ref[...]
    a = c * jax.nn.sigmoid(c)
    w = w_ref[0]
    a_hi = a.astype(BF16)
    a_lo = (a - a_hi.astype(F32)).astype(BF16)
    w_hi = w.astype(BF16)
    w_lo = (w - w_hi.astype(F32)).astype(BF16)
    acc = _dot(a_hi, w_hi) + _dot(a_lo, w_hi) + _dot(a_hi, w_lo)
    o_ref[0] = acc + b_ref[0]


def _modulation(cc, w_ada, b_ada):
    depth, d, n = w_ada.shape
    return pl.pallas_call(
        _mod_kernel,
        out_shape=jax.ShapeDtypeStruct((depth, MOD_ROWS, n), F32),
        grid=(depth, n // MOD_TN),
        in_specs=[
            pl.BlockSpec((MOD_ROWS, d), lambda l, j: (0, 0)),
            pl.BlockSpec((1, d, MOD_TN), lambda l, j: (l, 0, j)),
            pl.BlockSpec((1, 1, MOD_TN), lambda l, j: (l, 0, j)),
        ],
        out_specs=pl.BlockSpec((1, MOD_ROWS, MOD_TN), lambda l, j: (l, 0, j)),
        compiler_params=_params("parallel", "parallel"),
        name="modulation",
    )(cc, w_ada, b_ada.reshape(depth, 1, n))


def _mod_spec(row, k, d, per_batch):
    if per_batch:
        return pl.BlockSpec((1, 1, d), lambda b, i: (row + b, 0, k))
    return pl.BlockSpec((1, 1, d), lambda b, i: (row, 0, k))


def _in_proj_kernel(*refs, widths, rope_cols, q_cols, chunk):
    if rope_cols:
        x_ref, sh_ref, sc_ref, g_ref, w_ref, cos_ref, sin_ref = refs[:7]
        out_refs = refs[7:]
    else:
        x_ref, sh_ref, sc_ref, g_ref, w_ref = refs[:5]
        out_refs = refs[5:]
    x = x_ref[0]
    h = (_rms(x) * g_ref[...]) * (1.0 + sc_ref[0]) + sh_ref[0]
    hb = h.astype(BF16)
    if rope_cols:
        lane = lax.broadcasted_iota(jnp.int32, (1, V7X_LANES), 1)
        first_half = (lane % (2 * ROPE_FREQS)) < ROPE_FREQS
        cos = cos_ref[...]
        sin = sin_ref[...]
    col = 0
    for o_ref, width in zip(out_refs, widths):
        for c0 in range(0, width, chunk):
            p = _dot(hb, w_ref[:, col + c0:col + c0 + chunk])
            tiles = []
            for j in range(chunk // V7X_LANES):
                t = p[:, j * V7X_LANES:(j + 1) * V7X_LANES]
                g0 = col + c0 + j * V7X_LANES
                if g0 < rope_cols:
                    partner = jnp.where(first_half,
                                        pltpu.roll(t, V7X_LANES - ROPE_FREQS, 1),
                                        pltpu.roll(t, ROPE_FREQS, 1))
                    t = t * cos + partner * sin
                if g0 < q_cols:
                    t = t * (HEAD_DIM ** -0.5)
                tiles.append(t.astype(BF16))
            o_ref[0, :, c0:c0 + chunk] = jnp.concatenate(tiles, axis=-1)
        col += width


def _in_proj(x, mod, row, per_batch, g, w, tables, widths, rope_cols, q_cols, tm):
    b, t, d = x.shape
    n = w.shape[1]
    assert sum(widths) == n and t % tm == 0
    in_specs = [
        pl.BlockSpec((1, tm, d), lambda bb, i: (bb, i, 0)),
        _mod_spec(row, 0, d, per_batch),
        _mod_spec(row, 1, d, per_batch),
        pl.BlockSpec((1, d), lambda bb, i: (0, 0)),
        pl.BlockSpec((d, n), lambda bb, i: (0, 0)),
    ]
    args = [x, mod, mod, g.reshape(1, d), w]
    if rope_cols:
        in_specs += [pl.BlockSpec((tm, V7X_LANES), lambda bb, i: (i, 0))] * 2
        args += list(tables)
    kern = functools.partial(_in_proj_kernel, widths=widths, rope_cols=rope_cols,
                             q_cols=q_cols, chunk=2 * V7X_LANES)
    return pl.pallas_call(
        kern,
        out_shape=[jax.ShapeDtypeStruct((b, t, wd), BF16) for wd in widths],
        grid=(b, t // tm),
        in_specs=in_specs,
        out_specs=[pl.BlockSpec((1, tm, wd), lambda bb, i: (bb, i, 0)) for wd in widths],
        compiler_params=_params("parallel", "parallel"),
        name="in_proj",
    )(*args)


def _conv_kernel(r_ref, wcf_ref, bcf_ref, lng_ref, lnb_ref, wsc_ref, gc_ref, gs_ref, o_ref,
                 u_ref, v_ref, *, t, conf_w, sc_w, conf_k, rc):
    win = rc + 2 * CONV_PAD
    half = V7X_LANES
    zeros = jnp.zeros((CONV_PAD, conf_w), F32)
    u_ref[0:CONV_PAD, :] = zeros
    u_ref[CONV_PAD + t:, :] = zeros
    v_ref[0:CONV_PAD, :] = zeros
    v_ref[CONV_PAD + t:, :] = zeros

    def fill(c, carry):
        r0 = pl.multiple_of(c * rc, rc)
        cv = r_ref[0, pl.ds(r0, rc), 0:conf_w].astype(F32)
        cg = r_ref[0, pl.ds(r0, rc), conf_w:2 * conf_w].astype(F32)
        u_ref[pl.ds(CONV_PAD + r0, rc), :] = cv * jax.nn.sigmoid(cg)
        scg = r_ref[0, pl.ds(r0, rc), 2 * conf_w + sc_w:2 * conf_w + 2 * sc_w].astype(F32)
        su = r_ref[0, pl.ds(r0, rc), 2 * conf_w + 2 * sc_w:2 * conf_w + 3 * sc_w].astype(F32)
        v_ref[pl.ds(CONV_PAD + r0, rc), :] = scg * su
        return carry

    lax.fori_loop(0, t // rc, fill, 0)

    pad_c = conf_k // 2

    def body(c, carry):
        r0 = pl.multiple_of(c * rc, rc)
        conv_halves = []
        for hh in range(conf_w // half):
            lanes = slice(hh * half, (hh + 1) * half)
            wdw = wcf_ref[:, lanes]
            window = u_ref[pl.ds(r0, win), lanes]
            acc = jnp.zeros((rc, half), F32)
            for res in range(8):
                shifted = window if res == 0 else pltpu.roll(window, win - res, 0)
                for k in range(conf_k):
                    off = CONV_PAD - pad_c + k
                    if off % 8 != res:
                        continue
                    a0 = off - res
                    acc = acc + shifted[a0:a0 + rc, :] * wdw[k:k + 1, :]
            conv_halves.append(acc)
        conv = jnp.concatenate(conv_halves, axis=-1) + bcf_ref[...]
        mu = jnp.mean(conv, axis=-1, keepdims=True)
        cen = conv - mu
        var = jnp.mean(cen * cen, axis=-1, keepdims=True)
        ln = cen * lax.rsqrt(var + EPS) * lng_ref[...] + lnb_ref[...]
        cf = ln * jax.nn.sigmoid(ln)
        o_ref[0, pl.ds(r0, rc), 0:conf_w] = (_rms(cf) * gc_ref[...]).astype(BF16)

        vwin = v_ref[pl.ds(r0, win), :]
        wsc = wsc_ref[...]
        vm = pltpu.roll(vwin, 1, 0)[CONV_PAD:CONV_PAD + rc, :]
        vc = vwin[CONV_PAD:CONV_PAD + rc, :]
        vp = pltpu.roll(vwin, win - 1, 0)[CONV_PAD:CONV_PAD + rc, :]
        sb = r_ref[0, pl.ds(r0, rc), 2 * conf_w:2 * conf_w + sc_w].astype(F32)
        s = sb * (vm * wsc[0:1, :] + vc * wsc[1:2, :] + vp * wsc[2:3, :])
        o_ref[0, pl.ds(r0, rc), conf_w:conf_w + sc_w] = (_rms(s) * gs_ref[...]).astype(BF16)
        return carry

    lax.fori_loop(0, t // rc, body, 0)


def _conv_groups(r, w_conf_dw, b_conf_dw, ln_g, ln_b, w_sc_dw, g_conf, g_sc):
    b, t, rw = r.shape
    conf_k, conf_w = w_conf_dw.shape
    sc_w = w_sc_dw.shape[1]
    assert conf_w == sc_w and rw == 2 * conf_w + 3 * sc_w and conf_k // 2 < CONV_PAD
    rc = min(CONV_RC, t)
    const = lambda shape: pl.BlockSpec(shape, lambda bb: (0, 0))
    kern = functools.partial(_conv_kernel, t=t, conf_w=conf_w, sc_w=sc_w, conf_k=conf_k, rc=rc)
    return pl.pallas_call(
        kern,
        out_shape=jax.ShapeDtypeStruct((b, t, conf_w + sc_w), BF16),
        grid=(b,),
        in_specs=[
            pl.BlockSpec((1, t, rw), lambda bb: (bb, 0, 0)),
            const((conf_k, conf_w)), const((1, conf_w)), const((1, conf_w)), const((1, conf_w)),
            const(w_sc_dw.shape), const((1, conf_w)), const((1, sc_w)),
        ],
        out_specs=pl.BlockSpec((1, t, conf_w + sc_w), lambda bb: (bb, 0, 0)),
        scratch_shapes=[pltpu.VMEM((t + 2 * CONV_PAD, conf_w), F32),
                        pltpu.VMEM((t + 2 * CONV_PAD, sc_w), F32)],
        compiler_params=_params("parallel"),
        name="conv_groups",
    )(r, w_conf_dw, b_conf_dw.reshape(1, -1), ln_g.reshape(1, -1), ln_b.reshape(1, -1),
      w_sc_dw, g_conf.reshape(1, -1), g_sc.reshape(1, -1))


def _softmax_block(s_parts, sink):
    m = jnp.maximum(functools.reduce(
        jnp.maximum, [jnp.max(s, axis=-1, keepdims=True) for s in s_parts]), sink)
    ps = [jnp.exp(s - m) for s in s_parts]
    total = functools.reduce(lambda a, c: a + c,
                             [jnp.sum(p, axis=-1, keepdims=True) for p in ps]) + jnp.exp(sink - m)
    return ps, 1.0 / total


def _mix_kernel(*refs, band, t, tq, sink_base):
    if band:
        (sink_ref, q_ref, kk_ref, vv_ref, kc_ref, vc_ref, zcs_ref, x_ref, gt_ref, sc2_ref, sh2_ref,
         ga_ref, gpost_ref, gffn_ref, wout_ref, xo_ref, h2_ref) = refs
    else:
        (sink_ref, q_ref, kc_ref, vc_ref, zcs_ref, x_ref, gt_ref, sc2_ref, sh2_ref,
         ga_ref, gpost_ref, gffn_ref, wout_ref, xo_ref, h2_ref) = refs
    blk = ATTN_BLOCK
    band_w = blk + 2 * WINDOW
    lane = lax.broadcasted_iota(jnp.int32, (1, V7X_LANES), 1)
    m_lo = jnp.where(lane < HEAD_DIM, 1.0, 0.0).astype(BF16)
    m_hi = jnp.where(lane >= HEAD_DIM, 1.0, 0.0).astype(BF16)
    t0 = pl.program_id(1) * tq

    a_rows = []
    for qb in range(tq // blk):
        rows = slice(qb * blk, (qb + 1) * blk)
        if band:
            q0 = t0 + qb * blk
            start = pl.multiple_of(jnp.clip(q0 - WINDOW, 0, t - band_w), blk)
            rel = (lax.broadcasted_iota(jnp.int32, (blk, band_w), 1)
                   - lax.broadcasted_iota(jnp.int32, (blk, band_w), 0)) + (start - q0)
            visible = jnp.abs(rel) <= WINDOW
        a_tiles = []
        for h in range(N_KV_HEADS):
            kv_lanes = slice(h * V7X_LANES, (h + 1) * V7X_LANES)
            qp0 = q_ref[0, rows, (2 * h) * V7X_LANES:(2 * h + 1) * V7X_LANES]
            qp1 = q_ref[0, rows, (2 * h + 1) * V7X_LANES:(2 * h + 2) * V7X_LANES]
            qs = jnp.concatenate([qp0 * m_lo, qp1 * m_lo, qp0 * m_hi, qp1 * m_hi], axis=0)
            heads = (4 * h, 4 * h + 2, 4 * h + 1, 4 * h + 3)
            kc = kc_ref[0, :, kv_lanes]
            vc = vc_ref[0, :, kv_lanes]
            s_ctx = _dot_nt(qs, kc)
            if band:
                kb = kk_ref[0, pl.ds(start, band_w), kv_lanes]
                vb = vv_ref[0, pl.ds(start, band_w), kv_lanes]
                s_band = _dot_nt(qs, kb)
            p_ctx, p_band, inv = [], [], []
            for r, head in enumerate(heads):
                rr = slice(r * blk, (r + 1) * blk)
                parts = [s_ctx[rr]]
                if band:
                    parts.append(jnp.where(visible, s_band[rr], NEG))
                ps, inv_r = _softmax_block(parts, sink_ref[sink_base + head])
                p_ctx.append(ps[0].astype(BF16))
                if band:
                    p_band.append(ps[1].astype(BF16))
                inv.append(inv_r)
            pc_lo = jnp.concatenate(p_ctx[0:2], axis=0)
            pc_hi = jnp.concatenate(p_ctx[2:4], axis=0)
            o_lo = _dot(pc_lo, vc * m_lo)
            o_hi = _dot(pc_hi, vc * m_hi)
            if band:
                o_lo = o_lo + _dot(jnp.concatenate(p_band[0:2], axis=0), vb * m_lo)
                o_hi = o_hi + _dot(jnp.concatenate(p_band[2:4], axis=0), vb * m_hi)
            o = (o_lo * jnp.concatenate(inv[0:2], axis=0)
                 + o_hi * jnp.concatenate(inv[2:4], axis=0))
            a_tiles += [o[0:blk], o[blk:2 * blk]]
        a_rows.append(jnp.concatenate(a_tiles, axis=-1))
    a = jnp.concatenate(a_rows, axis=0) if len(a_rows) > 1 else a_rows[0]

    za = (_rms(a) * ga_ref[...]).astype(BF16)
    y = _dot(za, wout_ref[0:ATTN_W, :]) + _dot(zcs_ref[0], wout_ref[ATTN_W:, :])
    x_mid = x_ref[0] + gt_ref[0] * (_rms(y) * gpost_ref[...])
    xo_ref[0] = x_mid
    h2 = (_rms(x_mid) * gffn_ref[...]) * (1.0 + sc2_ref[0]) + sh2_ref[0]
    h2_ref[0] = h2.astype(BF16)


def _mix(sink, sink_base, q, kk, vv, kc, vc, zcs, x, mod, row, per_batch,
         g_attn, g_post, g_ffn, w_out, tq):
    b, t, d = x.shape
    band = kk is not None
    ctx_len = kc.shape[1]
    assert t % tq == 0 and tq % ATTN_BLOCK == 0
    tile = lambda w: pl.BlockSpec((1, tq, w), lambda bb, i: (bb, i, 0))
    whole = lambda n, w: pl.BlockSpec((1, n, w), lambda bb, i: (bb, 0, 0))
    const = lambda shape: pl.BlockSpec(shape, lambda bb, i: (0, 0))
    in_specs = [pl.BlockSpec(memory_space=pltpu.SMEM), tile(ATTN_W)]
    args = [sink, q]
    if band:
        in_specs += [whole(t, KV_DUP_W), whole(t, KV_DUP_W)]
        args += [kk, vv]
    in_specs += [whole(ctx_len, KV_DUP_W), whole(ctx_len, KV_DUP_W), tile(zcs.shape[2]), tile(d),
                 _mod_spec(row, 2, d, per_batch), _mod_spec(row, 4, d, per_batch),
                 _mod_spec(row, 3, d, per_batch),
                 const((1, ATTN_W)), const((1, d)), const((1, d)), const(w_out.shape)]
    args += [kc, vc, zcs, x, mod, mod, mod, g_attn.reshape(1, -1), g_post.reshape(1, -1),
             g_ffn.reshape(1, -1), w_out]
    kern = functools.partial(_mix_kernel, band=band, t=t, tq=tq, sink_base=sink_base)
    return pl.pallas_call(
        kern,
        out_shape=[jax.ShapeDtypeStruct((b, t, d), F32), jax.ShapeDtypeStruct((b, t, d), BF16)],
        grid=(b, t // tq),
        in_specs=in_specs,
        out_specs=[tile(d), tile(d)],
        compiler_params=_params("parallel", "parallel"),
        name="mix",
    )(*args)


def _ffn_kernel(hp_ref, h_ref, hn_ref, x_ref, gt_ref, gpost_ref, wup_ref, wdw_ref, wdn_ref, o_ref,
                hcat_ref, acc_ref, *, tm, d_ff, tf):
    i = pl.program_id(1)
    rows = tm + 2 * FFN_HALO
    hcat_ref[FFN_HALO:FFN_HALO + tm, :] = h_ref[0]

    @pl.when(i > 0)
    def _():
        hcat_ref[0:FFN_HALO, :] = hp_ref[0]

    @pl.when(i == 0)
    def _():
        hcat_ref[0:FFN_HALO, :] = jnp.zeros((FFN_HALO, hcat_ref.shape[1]), BF16)

    @pl.when(i < pl.num_programs(1) - 1)
    def _():
        hcat_ref[FFN_HALO + tm:, :] = hn_ref[0]

    @pl.when(i == pl.num_programs(1) - 1)
    def _():
        hcat_ref[FFN_HALO + tm:, :] = jnp.zeros((FFN_HALO, hcat_ref.shape[1]), BF16)

    hc = hcat_ref[...]
    mid = slice(FFN_HALO, FFN_HALO + tm)

    def conv3(u, w):
        return (pltpu.roll(u, 1, 0)[mid] * w[0:1, :] + u[mid] * w[1:2, :]
                + pltpu.roll(u, rows - 1, 0)[mid] * w[2:3, :])

    for c in range(d_ff // tf):
        gcols = slice(c * tf, (c + 1) * tf)
        vcols = slice(d_ff + c * tf, d_ff + (c + 1) * tf)
        gate = conv3(_dot(hc, wup_ref[:, gcols]), wdw_ref[:, gcols])
        val = conv3(_dot(hc, wup_ref[:, vcols]), wdw_ref[:, vcols])
        act = (gate * jax.nn.sigmoid(gate) * val).astype(BF16)
        contrib = _dot(act, wdn_ref[gcols, :])
        if c == 0:
            acc_ref[...] = contrib
        else:
            acc_ref[...] += contrib
    o_ref[0] = x_ref[0] + gt_ref[0] * (_rms(acc_ref[...]) * gpost_ref[...])


def _ffn(h2, x, mod, row, per_batch, g_post, w_up, w_dw, w_down, tm):
    b, t, d = x.shape
    d_ff = w_down.shape[0]
    assert t % tm == 0 and tm % FFN_HALO == 0 and d_ff % FFN_TF == 0
    per_tile = tm // FFN_HALO
    last = t // FFN_HALO - 1
    const = lambda shape: pl.BlockSpec(shape, lambda bb, i: (0, 0))
    tile = pl.BlockSpec((1, tm, d), lambda bb, i: (bb, i, 0))
    kern = functools.partial(_ffn_kernel, tm=tm, d_ff=d_ff, tf=FFN_TF)
    return pl.pallas_call(
        kern,
        out_shape=jax.ShapeDtypeStruct((b, t, d), F32),
        grid=(b, t // tm),
        in_specs=[
            pl.BlockSpec((1, FFN_HALO, d), lambda bb, i: (bb, jnp.maximum(i * per_tile - 1, 0), 0)),
            tile,
            pl.BlockSpec((1, FFN_HALO, d), lambda bb, i: (bb, jnp.minimum((i + 1) * per_tile, last), 0)),
            tile,
            _mod_spec(row, 5, d, per_batch),
            const((1, d)),
            pl.BlockSpec(w_up.shape, lambda bb, i: (0, 0), pipeline_mode=pl.Buffered(1)),
            const(w_dw.shape),
            pl.BlockSpec(w_down.shape, lambda bb, i: (0, 0), pipeline_mode=pl.Buffered(1)),
        ],
        out_specs=tile,
        scratch_shapes=[pltpu.VMEM((tm + 2 * FFN_HALO, d), BF16), pltpu.VMEM((tm, d), F32)],
        compiler_params=_params("parallel", "parallel"),
        name="conv_ffn",
    )(h2, h2, h2, x, mod, g_post.reshape(1, -1), w_up, w_dw, w_down)


def _rope_tables(n_tokens):
    lane = np.arange(V7X_LANES)
    axis = (lane % HEAD_DIM) // (2 * ROPE_FREQS)
    freq = lane % ROPE_FREQS
    sign = np.where((lane % (2 * ROPE_FREQS)) < ROPE_FREQS, -1.0, 1.0).astype(np.float32)
    tok = jnp.arange(n_tokens)
    pos = jnp.stack([tok // GRID_W, tok % GRID_W], axis=-1).astype(F32)
    inv = ROPE_THETA ** (-jnp.arange(ROPE_FREQS, dtype=F32) / ROPE_FREQS)
    ang = pos[:, axis] * inv[freq][None, :]
    return jnp.cos(ang), jnp.sin(ang) * sign[None, :]


def _augment_w_in(w_in_l):
    cols = [np.arange(0, ATTN_W)]
    for base in (ATTN_W, ATTN_W + KV_W):
        for h in range(N_KV_HEADS):
            head = np.arange(base + h * HEAD_DIM, base + (h + 1) * HEAD_DIM)
            cols += [head, head]
    cols.append(np.arange(ATTN_W + 2 * KV_W, w_in_l.shape[1]))
    return w_in_l[:, np.concatenate(cols)].astype(BF16)


def kernel(x, c, ctx, c_ctx, w_ada, b_ada, g_pre_mix, g_post_mix, g_pre_ffn, g_post_ffn, w_in, sink,
           w_conf_dw, b_conf_dw, conf_ln_g, conf_ln_b, w_sc_dw, g_group, w_out, w_up, w_ffn_dw, w_down):
    batch, seq, d = x.shape
    depth = w_in.shape[0]
    ctx_len = ctx.shape[1]
    conf_w = w_conf_dw.shape[2]
    sc_w = w_sc_dw.shape[2]
    r_w = 2 * conf_w + 3 * sc_w
    assert batch < MOD_ROWS

    cc = jnp.zeros((MOD_ROWS, d), F32).at[:batch].set(c).at[batch].set(c_ctx)
    mod = _modulation(cc, w_ada, b_ada).reshape(depth * MOD_ROWS, 1, N_MOD * d)
    tables = _rope_tables(seq)
    sink_flat = sink.reshape(-1).astype(F32)

    for l in range(depth):
        update_ctx = l < depth - 1
        row = l * MOD_ROWS
        w_in_l = _augment_w_in(w_in[l])
        w_out_l = w_out[l].astype(BF16)
        w_up_l = w_up[l].astype(BF16)
        w_down_l = w_down[l].astype(BF16)
        g_attn, g_conf, g_sc = (g_group[l, :ATTN_W], g_group[l, ATTN_W:ATTN_W + conf_w],
                                g_group[l, ATTN_W + conf_w:])
        conv_args = (w_conf_dw[l], b_conf_dw[l], conf_ln_g[l], conf_ln_b[l], w_sc_dw[l], g_conf, g_sc)

        q, kk, vv, r = _in_proj(x, mod, row, True, g_pre_mix[l], w_in_l, tables,
                                (ATTN_W, KV_DUP_W, KV_DUP_W, r_w), ATTN_W + KV_DUP_W, ATTN_W, IN_TM)
        if update_ctx:
            qc, kc, vc, rc = _in_proj(ctx, mod, row + batch, False, g_pre_mix[l], w_in_l, None,
                                      (ATTN_W, KV_DUP_W, KV_DUP_W, r_w), 0, ATTN_W, ctx_len)
        else:
            kc, vc = _in_proj(ctx, mod, row + batch, False, g_pre_mix[l],
                              w_in_l[:, ATTN_W:ATTN_W + 2 * KV_DUP_W], None,
                              (KV_DUP_W, KV_DUP_W), 0, 0, ctx_len)

        zcs = _conv_groups(r, *conv_args)
        x_mid, h2 = _mix(sink_flat, l * N_Q_HEADS, q, kk, vv, kc, vc, zcs, x, mod, row, True,
                         g_attn, g_post_mix[l], g_pre_ffn[l], w_out_l, MIX_TQ)
        x = _ffn(h2, x_mid, mod, row, True, g_post_ffn[l], w_up_l, w_ffn_dw[l], w_down_l, FFN_TM)

        if update_ctx:
            zcs_c = _conv_groups(rc, *conv_args)
            ctx_mid, hc2 = _mix(sink_flat, l * N_Q_HEADS, qc, None, None, kc, vc, zcs_c, ctx, mod,
                                row + batch, False, g_attn, g_post_mix[l], g_pre_ffn[l], w_out_l,
                                ctx_len)
            ctx = _ffn(hc2, ctx_mid, mod, row + batch, False, g_post_ffn[l], w_up_l, w_ffn_dw[l],
                       w_down_l, ctx_len)
    return x
```

```python
import functools

import numpy as np
import jax
import jax.numpy as jnp
from jax import lax
from jax.experimental import pallas as pl
from jax.experimental.pallas import tpu as pltpu

F32 = jnp.float32
BF16 = jnp.bfloat16

V7X_LANES = 128
V7X_BF16_SUBLANES = 16
V7X_VMEM_LIMIT_BYTES = 56 * 1024 * 1024

HEAD_DIM = 64
N_Q_HEADS = 8
N_KV_HEADS = 2
GROUP = N_Q_HEADS // N_KV_HEADS
GRID_W = 64
WINDOW = 128
ATTN_BLOCK = 128
ROPE_FREQS = HEAD_DIM // 4
ROPE_THETA = 10000.0
EPS = 1e-6
NEG = -1e30
LOG2E = 1.4426950408889634

ATTN_W = N_Q_HEADS * HEAD_DIM
KV_W = N_KV_HEADS * HEAD_DIM
KV_DUP_W = 2 * KV_W
N_MOD = 6
MOD_ROWS = 16

IN_TM = 512
MIX_TQ = 512
MIX_MERGE_ROWS = 256
FFN_TM = 512
FFN_TF = 256
FFN_HALO = V7X_BF16_SUBLANES
CONV_RC = 64
CONV_PAD = 16
MOD_TN = 1536


def _rms(x, eps=EPS):
    return x * lax.rsqrt(jnp.mean(x * x, axis=-1, keepdims=True) + eps)


def _dot(a, b):
    return jnp.dot(a, b, preferred_element_type=F32)


def _dot_nt(a, b):
    return lax.dot_general(a, b, (((1,), (1,)), ((), ())), preferred_element_type=F32)


def _params(*sem):
    return pltpu.CompilerParams(dimension_semantics=sem, vmem_limit_bytes=V7X_VMEM_LIMIT_BYTES)


def _mod_kernel(c_ref, w_ref, b_ref, o_ref):
    c = c_ref[...]
    a = c * jax.nn.sigmoid(c)
    w = w_ref[0]
    a_hi = a.astype(BF16)
    a_lo = (a - a_hi.astype(F32)).astype(BF16)
    w_hi = w.astype(BF16)
    w_lo = (w - w_hi.astype(F32)).astype(BF16)
    acc = _dot(a_hi, w_hi) + _dot(a_lo, w_hi) + _dot(a_hi, w_lo)
    o_ref[0] = acc + b_ref[0]


def _modulation(cc, w_ada, b_ada):
    depth, d, n = w_ada.shape
    return pl.pallas_call(
        _mod_kernel,
        out_shape=jax.ShapeDtypeStruct((depth, MOD_ROWS, n), F32),
        grid=(depth, n // MOD_TN),
        in_specs=[
            pl.BlockSpec((MOD_ROWS, d), lambda l, j: (0, 0)),
            pl.BlockSpec((1, d, MOD_TN), lambda l, j: (l, 0, j)),
            pl.BlockSpec((1, 1, MOD_TN), lambda l, j: (l, 0, j)),
        ],
        out_specs=pl.BlockSpec((1, MOD_ROWS, MOD_TN), lambda l, j: (l, 0, j)),
        compiler_params=_params("parallel", "parallel"),
        name="modulation",
    )(cc, w_ada, b_ada.reshape(depth, 1, n))


def _mod_spec(row, k, d, per_batch):
    if per_batch:
        return pl.BlockSpec((1, 1, d), lambda b, i: (row + b, 0, k))
    return pl.BlockSpec((1, 1, d), lambda b, i: (row, 0, k))


def _in_proj_kernel(*refs, outs, rope, chunk):
    if rope:
        x_ref, sh_ref, sc_ref, g_ref, w_ref, cos_ref, sin_ref = refs[:7]
        out_refs = refs[7:]
    else:
        x_ref, sh_ref, sc_ref, g_ref, w_ref = refs[:5]
        out_refs = refs[5:]
    x = x_ref[0]
    h = (_rms(x) * g_ref[...]) * (1.0 + sc_ref[0]) + sh_ref[0]
    hb = h.astype(BF16)
    lane = lax.broadcasted_iota(jnp.int32, (1, V7X_LANES), 1)
    lo_lanes = lane < HEAD_DIM
    if rope:
        first_half = (lane % (2 * ROPE_FREQS)) < ROPE_FREQS
        cos = cos_ref[...]
        sin = sin_ref[...]
    products = {}
    for o_ref, (col, width, kind) in zip(out_refs, outs):
        for c0 in range(0, width, chunk):
            cols = (col + c0, col + c0 + chunk)
            if cols not in products:
                products[cols] = _dot(hb, w_ref[:, cols[0]:cols[1]])
            p = products[cols]
            tiles = []
            for j in range(chunk // V7X_LANES):
                t = p[:, j * V7X_LANES:(j + 1) * V7X_LANES]
                if rope and kind in ("q", "k"):
                    partner = jnp.where(first_half,
                                        pltpu.roll(t, V7X_LANES - ROPE_FREQS, 1),
                                        pltpu.roll(t, ROPE_FREQS, 1))
                    t = t * cos + partner * sin
                if kind == "q":
                    t = t * (HEAD_DIM ** -0.5 * LOG2E)
                elif kind == "v_lo":
                    t = jnp.where(lo_lanes, t, 1.0)
                elif kind == "v_hi":
                    t = jnp.where(lo_lanes, 1.0, t)
                tiles.append(t.astype(BF16))
            o_ref[0, :, c0:c0 + chunk] = jnp.concatenate(tiles, axis=-1)


def _in_proj(x, mod, row, per_batch, g, w, tables, outs, tm):
    b, t, d = x.shape
    n = w.shape[1]
    rope = tables is not None
    widths = [o[1] for o in outs]
    assert t % tm == 0 and all(o[0] + o[1] <= n for o in outs)
    in_specs = [
        pl.BlockSpec((1, tm, d), lambda bb, i: (bb, i, 0)),
        _mod_spec(row, 0, d, per_batch),
        _mod_spec(row, 1, d, per_batch),
        pl.BlockSpec((1, d), lambda bb, i: (0, 0)),
        pl.BlockSpec((d, n), lambda bb, i: (0, 0)),
    ]
    args = [x, mod, mod, g.reshape(1, d), w]
    if rope:
        in_specs += [pl.BlockSpec((tm, V7X_LANES), lambda bb, i: (i, 0))] * 2
        args += list(tables)
    kern = functools.partial(_in_proj_kernel, outs=tuple(outs), rope=rope, chunk=2 * V7X_LANES)
    return pl.pallas_call(
        kern,
        out_shape=[jax.ShapeDtypeStruct((b, t, wd), BF16) for wd in widths],
        grid=(b, t // tm),
        in_specs=in_specs,
        out_specs=[pl.BlockSpec((1, tm, wd), lambda bb, i: (bb, i, 0)) for wd in widths],
        compiler_params=_params("parallel", "parallel"),
        name="in_proj",
    )(*args)


def _conv_kernel(r_ref, wcf_ref, bcf_ref, lng_ref, lnb_ref, wsc_ref, gc_ref, gs_ref, o_ref,
                 u_ref, v_ref, *, t, conf_w, sc_w, conf_k, rc):
    win = rc + 2 * CONV_PAD
    half = V7X_LANES
    zeros = jnp.zeros((CONV_PAD, conf_w), F32)
    u_ref[0:CONV_PAD, :] = zeros
    u_ref[CONV_PAD + t:, :] = zeros
    v_ref[0:CONV_PAD, :] = zeros
    v_ref[CONV_PAD + t:, :] = zeros

    def fill(c, carry):
        r0 = pl.multiple_of(c * rc, rc)
        cv = r_ref[0, pl.ds(r0, rc), 0:conf_w].astype(F32)
        cg = r_ref[0, pl.ds(r0, rc), conf_w:2 * conf_w].astype(F32)
        u_ref[pl.ds(CONV_PAD + r0, rc), :] = cv * jax.nn.sigmoid(cg)
        scg = r_ref[0, pl.ds(r0, rc), 2 * conf_w + sc_w:2 * conf_w + 2 * sc_w].astype(F32)
        su = r_ref[0, pl.ds(r0, rc), 2 * conf_w + 2 * sc_w:2 * conf_w + 3 * sc_w].astype(F32)
        v_ref[pl.ds(CONV_PAD + r0, rc), :] = scg * su
        return carry

    lax.fori_loop(0, t // rc, fill, 0)

    pad_c = conf_k // 2

    def body(c, carry):
        r0 = pl.multiple_of(c * rc, rc)
        conv_halves = []
        for hh in range(conf_w // half):
            lanes = slice(hh * half, (hh + 1) * half)
            wdw = wcf_ref[:, lanes]
            window = u_ref[pl.ds(r0, win), lanes]
            acc = jnp.zeros((rc, half), F32)
            for res in range(8):
                shifted = window if res == 0 else pltpu.roll(window, win - res, 0)
                for k in range(conf_k):
                    off = CONV_PAD - pad_c + k
                    if off % 8 != res:
                        continue
                    a0 = off - res
                    acc = acc + shifted[a0:a0 + rc, :] * wdw[k:k + 1, :]
            conv_halves.append(acc)
        conv = jnp.concatenate(conv_halves, axis=-1) + bcf_ref[...]
        mu = jnp.mean(conv, axis=-1, keepdims=True)
        cen = conv - mu
        var = jnp.mean(cen * cen, axis=-1, keepdims=True)
        ln = cen * lax.rsqrt(var + EPS) * lng_ref[...] + lnb_ref[...]
        cf = ln * jax.nn.sigmoid(ln)
        o_ref[0, pl.ds(r0, rc), 0:conf_w] = (_rms(cf) * gc_ref[...]).astype(BF16)

        vwin = v_ref[pl.ds(r0, win), :]
        wsc = wsc_ref[...]
        vm = pltpu.roll(vwin, 1, 0)[CONV_PAD:CONV_PAD + rc, :]
        vc = vwin[CONV_PAD:CONV_PAD + rc, :]
        vp = pltpu.roll(vwin, win - 1, 0)[CONV_PAD:CONV_PAD + rc, :]
        sb = r_ref[0, pl.ds(r0, rc), 2 * conf_w:2 * conf_w + sc_w].astype(F32)
        s = sb * (vm * wsc[0:1, :] + vc * wsc[1:2, :] + vp * wsc[2:3, :])
        o_ref[0, pl.ds(r0, rc), conf_w:conf_w + sc_w] = (_rms(s) * gs_ref[...]).astype(BF16)
        return carry

    lax.fori_loop(0, t // rc, body, 0)


def _conv_groups(r, w_conf_dw, b_conf_dw, ln_g, ln_b, w_sc_dw, g_conf, g_sc):
    b, t, rw = r.shape
    conf_k, conf_w = w_conf_dw.shape
    sc_w = w_sc_dw.shape[1]
    assert conf_w == sc_w and rw == 2 * conf_w + 3 * sc_w and conf_k // 2 < CONV_PAD
    rc = min(CONV_RC, t)
    const = lambda shape: pl.BlockSpec(shape, lambda bb: (0, 0))
    kern = functools.partial(_conv_kernel, t=t, conf_w=conf_w, sc_w=sc_w, conf_k=conf_k, rc=rc)
    return pl.pallas_call(
        kern,
        out_shape=jax.ShapeDtypeStruct((b, t, conf_w + sc_w), BF16),
        grid=(b,),
        in_specs=[
            pl.BlockSpec((1, t, rw), lambda bb: (bb, 0, 0)),
            const((conf_k, conf_w)), const((1, conf_w)), const((1, conf_w)), const((1, conf_w)),
            const(w_sc_dw.shape), const((1, conf_w)), const((1, sc_w)),
        ],
        out_specs=pl.BlockSpec((1, t, conf_w + sc_w), lambda bb: (bb, 0, 0)),
        scratch_shapes=[pltpu.VMEM((t + 2 * CONV_PAD, conf_w), F32),
                        pltpu.VMEM((t + 2 * CONV_PAD, sc_w), F32)],
        compiler_params=_params("parallel"),
        name="conv_groups",
    )(r, w_conf_dw, b_conf_dw.reshape(1, -1), ln_g.reshape(1, -1), ln_b.reshape(1, -1),
      w_sc_dw, g_conf.reshape(1, -1), g_sc.reshape(1, -1))


def _mix_kernel(*refs, band, t, tq, sink_base):
    if band:
        (sink_ref, q_ref, kk_ref, vlo_ref, vhi_ref, kc_ref, vclo_ref, vchi_ref, zcs_ref, x_ref,
         gt_ref, sc2_ref, sh2_ref, ga_ref, gpost_ref, gffn_ref, wout_ref, xo_ref, h2_ref) = refs
    else:
        (sink_ref, q_ref, kc_ref, vclo_ref, vchi_ref, zcs_ref, x_ref,
         gt_ref, sc2_ref, sh2_ref, ga_ref, gpost_ref, gffn_ref, wout_ref, xo_ref, h2_ref) = refs
    blk = ATTN_BLOCK
    band_w = blk + 2 * WINDOW
    lane = lax.broadcasted_iota(jnp.int32, (1, V7X_LANES), 1)
    lo_lanes = lane < HEAD_DIM
    m_lo = jnp.where(lo_lanes, 1.0, 0.0).astype(BF16)
    m_hi = jnp.where(lo_lanes, 0.0, 1.0).astype(BF16)
    t0 = pl.program_id(1) * tq
    n_blocks = tq // blk
    groups = [(qb, h) for qb in range(n_blocks) for h in range(N_KV_HEADS)]

    def window(qb):
        q0 = t0 + qb * blk
        return q0, pl.multiple_of(jnp.clip(q0 - WINDOW, 0, t - band_w), blk)

    def scores(qb, h):
        rows = slice(qb * blk, (qb + 1) * blk)
        kv_lanes = slice(h * V7X_LANES, (h + 1) * V7X_LANES)
        qp0 = q_ref[0, rows, (2 * h) * V7X_LANES:(2 * h + 1) * V7X_LANES]
        qp1 = q_ref[0, rows, (2 * h + 1) * V7X_LANES:(2 * h + 2) * V7X_LANES]
        qs = jnp.concatenate([qp0 * m_lo, qp1 * m_lo, qp0 * m_hi, qp1 * m_hi], axis=0)
        s_ctx = _dot_nt(qs, kc_ref[0, :, kv_lanes])
        s_band = None
        if band:
            _, start = window(qb)
            s_band = _dot_nt(qs, kk_ref[0, pl.ds(start, band_w), kv_lanes])
        return s_ctx, s_band

    visible = {}

    def attend(qb, h, s_ctx, s_band):
        kv_lanes = slice(h * V7X_LANES, (h + 1) * V7X_LANES)
        heads = (4 * h, 4 * h + 2, 4 * h + 1, 4 * h + 3)
        if band:
            q0, start = window(qb)
            if qb not in visible:
                rel = (lax.broadcasted_iota(jnp.int32, (blk, band_w), 1)
                       - lax.broadcasted_iota(jnp.int32, (blk, band_w), 0)) + (start - q0)
                visible[qb] = jnp.abs(rel) <= WINDOW
        p_ctx, p_band, snk = [], [], []
        for r, head in enumerate(heads):
            rr = slice(r * blk, (r + 1) * blk)
            sink2 = sink_ref[sink_base + head] * LOG2E
            sc = s_ctx[rr]
            m = jnp.maximum(jnp.max(sc, axis=-1, keepdims=True), sink2)
            if band:
                sb = jnp.where(visible[qb], s_band[rr], NEG)
                m = jnp.maximum(m, jnp.max(sb, axis=-1, keepdims=True))
                p_band.append(jnp.exp2(sb - m).astype(BF16))
            p_ctx.append(jnp.exp2(sc - m).astype(BF16))
            snk.append(jnp.exp2(sink2 - m))
        o_lo = _dot(jnp.concatenate(p_ctx[0:2], axis=0), vclo_ref[0, :, kv_lanes])
        o_hi = _dot(jnp.concatenate(p_ctx[2:4], axis=0), vchi_ref[0, :, kv_lanes])
        if band:
            o_lo = o_lo + _dot(jnp.concatenate(p_band[0:2], axis=0),
                               vlo_ref[0, pl.ds(start, band_w), kv_lanes])
            o_hi = o_hi + _dot(jnp.concatenate(p_band[2:4], axis=0),
                               vhi_ref[0, pl.ds(start, band_w), kv_lanes])
        num = jnp.where(lo_lanes, o_lo, o_hi)
        den = jnp.where(lo_lanes,
                        pltpu.roll(o_lo, HEAD_DIM, 1) + jnp.concatenate(snk[0:2], axis=0),
                        pltpu.roll(o_hi, HEAD_DIM, 1) + jnp.concatenate(snk[2:4], axis=0))
        o = num / den
        return [o[0:blk], o[blk:2 * blk]]

    merge_rows = min(tq, MIX_MERGE_ROWS)
    blocks_per_merge = merge_rows // blk
    post_gain = gt_ref[0] * gpost_ref[...]
    ffn_gain = gffn_ref[...] * (1.0 + sc2_ref[0])

    def merge(mi, tiles):
        rows = slice(mi * merge_rows, (mi + 1) * merge_rows)
        a = jnp.concatenate(
            [jnp.concatenate([tl for h in range(N_KV_HEADS) for tl in tiles[(qb, h)]], axis=-1)
             for qb in range(mi * blocks_per_merge, (mi + 1) * blocks_per_merge)], axis=0)
        za = (_rms(a) * ga_ref[...]).astype(BF16)
        y = _dot(za, wout_ref[0:ATTN_W, :]) + _dot(zcs_ref[0, rows, :], wout_ref[ATTN_W:, :])
        x_mid = x_ref[0, rows, :] + _rms(y) * post_gain
        xo_ref[0, rows, :] = x_mid
        h2_ref[0, rows, :] = (_rms(x_mid) * ffn_gain + sh2_ref[0]).astype(BF16)

    tiles = {}
    nxt = scores(*groups[0])
    for gi, (qb, h) in enumerate(groups):
        cur = nxt
        if gi + 1 < len(groups):
            nxt = scores(*groups[gi + 1])
        tiles[(qb, h)] = attend(qb, h, *cur)
        if h == N_KV_HEADS - 1 and (qb + 1) % blocks_per_merge == 0:
            merge(qb // blocks_per_merge, tiles)


def _mix(sink, sink_base, q, kk, vlo, vhi, kc, vclo, vchi, zcs, x, mod, row, per_batch,
         g_attn, g_post, g_ffn, w_out, tq):
    b, t, d = x.shape
    band = kk is not None
    ctx_len = kc.shape[1]
    assert t % tq == 0 and tq % min(tq, MIX_MERGE_ROWS) == 0 and MIX_MERGE_ROWS % ATTN_BLOCK == 0
    tile = lambda w: pl.BlockSpec((1, tq, w), lambda bb, i: (bb, i, 0))
    whole = lambda n, w: pl.BlockSpec((1, n, w), lambda bb, i: (bb, 0, 0))
    const = lambda shape: pl.BlockSpec(shape, lambda bb, i: (0, 0))
    in_specs = [pl.BlockSpec(memory_space=pltpu.SMEM), tile(ATTN_W)]
    args = [sink, q]
    if band:
        in_specs += [whole(t, KV_DUP_W)] * 3
        args += [kk, vlo, vhi]
    in_specs += [whole(ctx_len, KV_DUP_W)] * 3
    in_specs += [tile(zcs.shape[2]), tile(d),
                 _mod_spec(row, 2, d, per_batch), _mod_spec(row, 4, d, per_batch),
                 _mod_spec(row, 3, d, per_batch),
                 const((1, ATTN_W)), const((1, d)), const((1, d)), const(w_out.shape)]
    args += [kc, vclo, vchi, zcs, x, mod, mod, mod, g_attn.reshape(1, -1), g_post.reshape(1, -1),
             g_ffn.reshape(1, -1), w_out]
    kern = functools.partial(_mix_kernel, band=band, t=t, tq=tq, sink_base=sink_base)
    return pl.pallas_call(
        kern,
        out_shape=[jax.ShapeDtypeStruct((b, t, d), F32), jax.ShapeDtypeStruct((b, t, d), BF16)],
        grid=(b, t // tq),
        in_specs=in_specs,
        out_specs=[tile(d), tile(d)],
        compiler_params=_params("parallel", "parallel"),
        name="mix",
    )(*args)


def _ffn_kernel(hp_ref, h_ref, hn_ref, x_ref, gt_ref, gpost_ref, wup_ref, wdw_ref, wdn_ref, o_ref,
                hcat_ref, act_ref, *, tm, d_ff, tf):
    i = pl.program_id(1)
    rows = tm + 2 * FFN_HALO
    hcat_ref[FFN_HALO:FFN_HALO + tm, :] = h_ref[0]

    @pl.when(i > 0)
    def _():
        hcat_ref[0:FFN_HALO, :] = hp_ref[0]

    @pl.when(i == 0)
    def _():
        hcat_ref[0:FFN_HALO, :] = jnp.zeros((FFN_HALO, hcat_ref.shape[1]), BF16)

    @pl.when(i < pl.num_programs(1) - 1)
    def _():
        hcat_ref[FFN_HALO + tm:, :] = hn_ref[0]

    @pl.when(i == pl.num_programs(1) - 1)
    def _():
        hcat_ref[FFN_HALO + tm:, :] = jnp.zeros((FFN_HALO, hcat_ref.shape[1]), BF16)

    hc = hcat_ref[...]
    mid = slice(FFN_HALO, FFN_HALO + tm)

    def conv3(u, w):
        return (pltpu.roll(u, 1, 0)[mid] * w[0:1, :] + u[mid] * w[1:2, :]
                + pltpu.roll(u, rows - 1, 0)[mid] * w[2:3, :])

    n_chunks = d_ff // tf

    def up(c):
        gcols = slice(c * tf, (c + 1) * tf)
        vcols = slice(d_ff + c * tf, d_ff + (c + 1) * tf)
        return _dot(hc, wup_ref[:, gcols]), _dot(hc, wup_ref[:, vcols])

    nxt = up(0)
    for c in range(n_chunks):
        ug, uv = nxt
        if c + 1 < n_chunks:
            nxt = up(c + 1)
        gcols = slice(c * tf, (c + 1) * tf)
        vcols = slice(d_ff + c * tf, d_ff + (c + 1) * tf)
        gate = conv3(ug, wdw_ref[:, gcols])
        val = conv3(uv, wdw_ref[:, vcols])
        act_ref[:, gcols] = (gate * jax.nn.sigmoid(gate) * val).astype(BF16)
    y = _dot(act_ref[...], wdn_ref[...])
    o_ref[0] = x_ref[0] + gt_ref[0] * (_rms(y) * gpost_ref[...])


def _ffn(h2, x, mod, row, per_batch, g_post, w_up, w_dw, w_down, tm):
    b, t, d = x.shape
    d_ff = w_down.shape[0]
    assert t % tm == 0 and tm % FFN_HALO == 0 and d_ff % FFN_TF == 0
    per_tile = tm // FFN_HALO
    last = t // FFN_HALO - 1
    const = lambda shape: pl.BlockSpec(shape, lambda bb, i: (0, 0))
    tile = pl.BlockSpec((1, tm, d), lambda bb, i: (bb, i, 0))
    kern = functools.partial(_ffn_kernel, tm=tm, d_ff=d_ff, tf=FFN_TF)
    return pl.pallas_call(
        kern,
        out_shape=jax.ShapeDtypeStruct((b, t, d), F32),
        grid=(b, t // tm),
        in_specs=[
            pl.BlockSpec((1, FFN_HALO, d), lambda bb, i: (bb, jnp.maximum(i * per_tile - 1, 0), 0)),
            tile,
            pl.BlockSpec((1, FFN_HALO, d), lambda bb, i: (bb, jnp.minimum((i + 1) * per_tile, last), 0)),
            tile,
            _mod_spec(row, 5, d, per_batch),
            const((1, d)),
            pl.BlockSpec(w_up.shape, lambda bb, i: (0, 0), pipeline_mode=pl.Buffered(1)),
            const(w_dw.shape),
            pl.BlockSpec(w_down.shape, lambda bb, i: (0, 0), pipeline_mode=pl.Buffered(1)),
        ],
        out_specs=tile,
        scratch_shapes=[pltpu.VMEM((tm + 2 * FFN_HALO, d), BF16), pltpu.VMEM((tm, d_ff), BF16)],
        compiler_params=_params("parallel", "parallel"),
        name="conv_ffn",
    )(h2, h2, h2, x, mod, g_post.reshape(1, -1), w_up, w_dw, w_down)


def _rope_tables(n_tokens):
    lane = np.arange(V7X_LANES)
    axis = (lane % HEAD_DIM) // (2 * ROPE_FREQS)
    freq = lane % ROPE_FREQS
    sign = np.where((lane % (2 * ROPE_FREQS)) < ROPE_FREQS, -1.0, 1.0).astype(np.float32)
    tok = jnp.arange(n_tokens)
    pos = jnp.stack([tok // GRID_W, tok % GRID_W], axis=-1).astype(F32)
    inv = ROPE_THETA ** (-jnp.arange(ROPE_FREQS, dtype=F32) / ROPE_FREQS)
    ang = pos[:, axis] * inv[freq][None, :]
    return jnp.cos(ang), jnp.sin(ang) * sign[None, :]


def _augment_w_in(w_in_l):
    cols = [np.arange(0, ATTN_W)]
    for base in (ATTN_W, ATTN_W + KV_W):
        for h in range(N_KV_HEADS):
            head = np.arange(base + h * HEAD_DIM, base + (h + 1) * HEAD_DIM)
            cols += [head, head]
    cols.append(np.arange(ATTN_W + 2 * KV_W, w_in_l.shape[1]))
    return w_in_l[:, np.concatenate(cols)].astype(BF16)


def kernel(x, c, ctx, c_ctx, w_ada, b_ada, g_pre_mix, g_post_mix, g_pre_ffn, g_post_ffn, w_in, sink,
           w_conf_dw, b_conf_dw, conf_ln_g, conf_ln_b, w_sc_dw, g_group, w_out, w_up, w_ffn_dw, w_down):
    batch, seq, d = x.shape
    depth = w_in.shape[0]
    ctx_len = ctx.shape[1]
    conf_w = w_conf_dw.shape[2]
    sc_w = w_sc_dw.shape[2]
    r_w = 2 * conf_w + 3 * sc_w
    assert batch < MOD_ROWS

    cc = jnp.zeros((MOD_ROWS, d), F32).at[:batch].set(c).at[batch].set(c_ctx)
    mod = _modulation(cc, w_ada, b_ada).reshape(depth * MOD_ROWS, 1, N_MOD * d)
    tables = _rope_tables(seq)
    sink_flat = sink.reshape(-1).astype(F32)

    for l in range(depth):
        update_ctx = l < depth - 1
        row = l * MOD_ROWS
        w_in_l = _augment_w_in(w_in[l])
        w_out_l = w_out[l].astype(BF16)
        w_up_l = w_up[l].astype(BF16)
        w_down_l = w_down[l].astype(BF16)
        g_attn, g_conf, g_sc = (g_group[l, :ATTN_W], g_group[l, ATTN_W:ATTN_W + conf_w],
                                g_group[l, ATTN_W + conf_w:])
        conv_args = (w_conf_dw[l], b_conf_dw[l], conf_ln_g[l], conf_ln_b[l], w_sc_dw[l], g_conf, g_sc)

        k_col, v_col, r_col = ATTN_W, ATTN_W + KV_DUP_W, ATTN_W + 2 * KV_DUP_W
        kv_outs = [(k_col, KV_DUP_W, "k"), (v_col, KV_DUP_W, "v_lo"), (v_col, KV_DUP_W, "v_hi")]
        all_outs = [(0, ATTN_W, "q")] + kv_outs + [(r_col, r_w, "plain")]
        q, kk, vlo, vhi, r = _in_proj(x, mod, row, True, g_pre_mix[l], w_in_l, tables, all_outs, IN_TM)
        if update_ctx:
            qc, kc, vclo, vchi, rc = _in_proj(ctx, mod, row + batch, False, g_pre_mix[l], w_in_l, None,
                                              all_outs, ctx_len)
        else:
            kc, vclo, vchi = _in_proj(ctx, mod, row + batch, False, g_pre_mix[l], w_in_l, None,
                                      kv_outs, ctx_len)

        zcs = _conv_groups(r, *conv_args)
        x_mid, h2 = _mix(sink_flat, l * N_Q_HEADS, q, kk, vlo, vhi, kc, vclo, vchi, zcs, x, mod, row,
                         True, g_attn, g_post_mix[l], g_pre_ffn[l], w_out_l, MIX_TQ)
        x = _ffn(h2, x_mid, mod, row, True, g_post_ffn[l], w_up_l, w_ffn_dw[l], w_down_l, FFN_TM)

        if update_ctx:
            zcs_c = _conv_groups(rc, *conv_args)
            ctx_mid, hc2 = _mix(sink_flat, l * N_Q_HEADS, qc, None, None, None, kc, vclo, vchi, zcs_c,
                                ctx, mod, row + batch, False, g_attn, g_post_mix[l], g_pre_ffn[l],
                                w_out_l, ctx_len)
            ctx = _ffn(hc2, ctx_mid, mod, row + batch, False, g_post_ffn[l], w_up_l, w_ffn_dw[l],
                       w_down_l, ctx_len)
    return x
```

```python
import functools

import numpy as np
import jax
import jax.numpy as jnp
from jax import lax
from jax.experimental import pallas as pl
from jax.experimental.pallas import tpu as pltpu

F32 = jnp.float32
BF16 = jnp.bfloat16

V7X_LANES = 128
V7X_SUBLANES = 8
V7X_BF16_SUBLANES = 16
V7X_VMEM_LIMIT_BYTES = 56 * 1024 * 1024

HEAD_DIM = 64
N_Q_HEADS = 8
N_KV_HEADS = 2
GRID_W = 64
WINDOW = 128
ATTN_BLOCK = 128
ROPE_FREQS = HEAD_DIM // 4
ROPE_THETA = 10000.0
EPS = 1e-6
NEG = -1e30
LOG2E = 1.4426950408889634

ATTN_W = N_Q_HEADS * HEAD_DIM
KV_W = N_KV_HEADS * HEAD_DIM
KV_DUP_W = 2 * KV_W
N_MOD = 6
MOD_ROWS = 16

IN_TM = 512
MIX_TQ = 512
MIX_MERGE_ROWS = 256
FFN_TM = 512
FFN_TF = 256
HALO = V7X_BF16_SUBLANES
CONV_RC = 64
PROJ_CHUNK = 2 * V7X_LANES
MOD_TN = 1536


def _rms(x, eps=EPS):
    return x * lax.rsqrt(jnp.mean(x * x, axis=-1, keepdims=True) + eps)


def _dot(a, b):
    return jnp.dot(a, b, preferred_element_type=F32)


def _dot_nt(a, b):
    return lax.dot_general(a, b, (((1,), (1,)), ((), ())), preferred_element_type=F32)


def _params(*sem):
    return pltpu.CompilerParams(dimension_semantics=sem, vmem_limit_bytes=V7X_VMEM_LIMIT_BYTES)


def _layer_spec(arr, l, cols=None, col_block=0, single_buffer=False):
    _, rows, width = arr.shape
    width = cols or width
    mode = dict(pipeline_mode=pl.Buffered(1)) if single_buffer else {}
    return pl.BlockSpec((1, rows, width), lambda *_: (l, 0, col_block), **mode)


def _mod_spec(row, k, d, per_batch):
    if per_batch:
        return pl.BlockSpec((1, 1, d), lambda b, i: (row + b, 0, k))
    return pl.BlockSpec((1, 1, d), lambda b, i: (row, 0, k))


def _halo_specs(tm, t, d):
    per_tile = tm // HALO
    last = t // HALO - 1
    return (pl.BlockSpec((1, HALO, d), lambda bb, i: (bb, jnp.maximum(i * per_tile - 1, 0), 0)),
            pl.BlockSpec((1, HALO, d), lambda bb, i: (bb, jnp.minimum((i + 1) * per_tile, last), 0)))


def _mod_kernel(c_ref, w_ref, b_ref, o_ref):
    c = c_ref[...]
    a = c * jax.nn.sigmoid(c)
    w = w_ref[0]
    a_hi = a.astype(BF16)
    a_lo = (a - a_hi.astype(F32)).astype(BF16)
    w_hi = w.astype(BF16)
    w_lo = (w - w_hi.astype(F32)).astype(BF16)
    acc = _dot(a_hi, w_hi) + _dot(a_lo, w_hi) + _dot(a_hi, w_lo)
    o_ref[0] = acc + b_ref[0]


def _modulation(cc, w_ada, b_ada):
    depth, d, n = w_ada.shape
    return pl.pallas_call(
        _mod_kernel,
        out_shape=jax.ShapeDtypeStruct((depth, MOD_ROWS, n), F32),
        grid=(depth, n // MOD_TN),
        in_specs=[
            pl.BlockSpec((MOD_ROWS, d), lambda l, j: (0, 0)),
            pl.BlockSpec((1, d, MOD_TN), lambda l, j: (l, 0, j)),
            pl.BlockSpec((1, 1, MOD_TN), lambda l, j: (l, 0, j)),
        ],
        out_specs=pl.BlockSpec((1, MOD_ROWS, MOD_TN), lambda l, j: (l, 0, j)),
        compiler_params=_params("parallel", "parallel"),
        name="modulation",
    )(cc, w_ada, b_ada.reshape(depth, 1, n))


def _project(h_ref, rows, w_ref, out_refs, outs, rope, cos, sin):
    lane = lax.broadcasted_iota(jnp.int32, (1, V7X_LANES), 1)
    lo_lanes = lane < HEAD_DIM
    first_half = (lane % (2 * ROPE_FREQS)) < ROPE_FREQS
    chunks = {}
    for o_ref, (col, width, kind) in zip(out_refs, outs):
        for c0 in range(0, width, PROJ_CHUNK):
            chunks.setdefault((col + c0, col + c0 + PROJ_CHUNK), []).append((o_ref, c0, kind))
    prev = None
    for cols, users in list(chunks.items()) + [(None, None)]:
        cur = (_dot(h_ref[rows, :], w_ref[0, :, cols[0]:cols[1]]), users) if users else None
        if prev is not None:
            p, p_users = prev
            for o_ref, c0, kind in p_users:
                tiles = []
                for j in range(PROJ_CHUNK // V7X_LANES):
                    t = p[:, j * V7X_LANES:(j + 1) * V7X_LANES]
                    if rope and kind in ("q", "k"):
                        partner = jnp.where(first_half,
                                            pltpu.roll(t, V7X_LANES - ROPE_FREQS, 1),
                                            pltpu.roll(t, ROPE_FREQS, 1))
                        t = t * cos + partner * sin
                    if kind == "q":
                        t = t * (HEAD_DIM ** -0.5 * LOG2E)
                    elif kind == "v_lo":
                        t = jnp.where(lo_lanes, t, 1.0)
                    elif kind == "v_hi":
                        t = jnp.where(lo_lanes, 1.0, t)
                    tiles.append(t.astype(BF16))
                o_ref[0, :, c0:c0 + PROJ_CHUNK] = jnp.concatenate(tiles, axis=-1)
        prev = cur
        yield None if cur is None else cur[0][0:V7X_SUBLANES, 0:V7X_LANES]


def _mixer_in_kernel(*refs, outs, rope, conv, r_col, tm, conf_k, cw):
    it = iter(refs)
    if conv:
        xp_ref, x_ref, xn_ref = next(it), next(it), next(it)
    else:
        x_ref = next(it)
    sh_ref, sc_ref, g_ref, w_ref = next(it), next(it), next(it), next(it)
    cos = sin = None
    if rope:
        cos, sin = next(it)[...], next(it)[...]
    if conv:
        wcf_ref, bcf_ref, lng_ref, lnb_ref, wsc_ref, gc_ref, gs_ref = (next(it) for _ in range(7))
    out_refs = [next(it) for _ in outs]
    if conv:
        zcs_ref, h_ref, u_ref, v_ref = next(it), next(it), next(it), next(it)
    else:
        h_ref = next(it)

    gain = g_ref[0] * (1.0 + sc_ref[0])
    shift = sh_ref[0]

    def modulate(xv):
        return (_rms(xv) * gain + shift).astype(BF16)

    if not conv:
        h_ref[...] = modulate(x_ref[0])
        for _ in _project(h_ref, slice(0, tm), w_ref, out_refs, outs, rope, cos, sin):
            pass
        return

    i = pl.program_id(1)
    top_ok = (i > 0).astype(F32)
    bot_ok = (i < pl.num_programs(1) - 1).astype(F32)
    mid = slice(HALO, HALO + tm)
    top = slice(0, HALO)
    bot = slice(HALO + tm, HALO + tm + HALO)
    h_ref[mid, :] = modulate(x_ref[0])
    h_ref[top, :] = modulate(xp_ref[0])
    h_ref[bot, :] = modulate(xn_ref[0])

    def rdot(k):
        return _dot(h_ref[...], w_ref[0, :, r_col + k * cw:r_col + (k + 1) * cw])

    def store_padded(ref, val):
        ref[mid, :] = val[mid]
        ref[top, :] = val[top] * top_ok
        ref[bot, :] = val[bot] * bot_ok

    cv, cg = rdot(0), rdot(1)
    store_padded(u_ref, cv * jax.nn.sigmoid(cg))

    rc = min(CONV_RC, tm)
    win = rc + 2 * HALO
    pad_c = conf_k // 2

    def conformer_chunk(c, after):
        r0 = c * rc
        zero = None
        if after is not None:
            half = jnp.uint32(16)
            bits = lax.shift_right_logical(lax.shift_right_logical(pltpu.bitcast(after, jnp.uint32), half), half)
            zero = pltpu.bitcast(bits, F32)[0:1, 0:1]
        halves = []
        for hh in range(cw // V7X_LANES):
            lanes = slice(hh * V7X_LANES, (hh + 1) * V7X_LANES)
            wdw = wcf_ref[0, :, lanes]
            window = u_ref[r0:r0 + win, lanes]
            acc = None
            for res in range(V7X_SUBLANES):
                shifted = window if res == 0 else pltpu.roll(window, win - res, 0)
                for k in range(conf_k):
                    off = HALO - pad_c + k
                    if off % V7X_SUBLANES != res:
                        continue
                    term = shifted[off - res:off - res + rc, :] * wdw[k:k + 1, :]
                    acc = term if acc is None else acc + term
            halves.append(acc)
        bias = bcf_ref[0] if zero is None else bcf_ref[0] + zero
        y = jnp.concatenate(halves, axis=-1) + bias
        cen = y - jnp.mean(y, axis=-1, keepdims=True)
        var = jnp.mean(cen * cen, axis=-1, keepdims=True)
        ln = cen * lax.rsqrt(var + EPS) * lng_ref[0] + lnb_ref[0]
        cf = ln * jax.nn.sigmoid(ln)
        zcs_ref[0, r0:r0 + rc, 0:cw] = (_rms(cf) * gc_ref[0]).astype(BF16)

    def short_conv(sb):
        vwin = v_ref[...]
        wsc = wsc_ref[0]
        s = sb[mid] * (pltpu.roll(vwin, 1, 0)[mid] * wsc[0:1, :] + vwin[mid] * wsc[1:2, :]
                       + pltpu.roll(vwin, tm + 2 * HALO - 1, 0)[mid] * wsc[2:3, :])
        zcs_ref[0, :, cw:2 * cw] = (_rms(s) * gs_ref[0]).astype(BF16)

    piece = lambda val: val[0:V7X_SUBLANES, 0:V7X_LANES]
    short_in = {}

    def mxu_work():
        short_in["sb"] = rdot(2)
        yield piece(short_in["sb"])
        scg = rdot(3)
        yield piece(scg)
        su = rdot(4)
        store_padded(v_ref, scg * su)
        yield piece(su)
        yield from _project(h_ref, mid, w_ref, out_refs, outs, rope, cos, sin)

    work = mxu_work()
    pending = None
    for c in range(tm // rc):
        after, pending = pending, next(work, None)
        conformer_chunk(c, after)
    for _ in work:
        pass
    short_conv(short_in["sb"])


def _mixer_in(x, mod, row, per_batch, g_pre, w, l, tables, outs, conv_params, tm):
    b, t, d = x.shape
    n = w.shape[2]
    rope = tables is not None
    conv = conv_params is not None
    widths = [o[1] for o in outs]
    assert t % tm == 0 and tm % HALO == 0 and all(o[0] + o[1] <= n for o in outs)
    tile = lambda wd: pl.BlockSpec((1, tm, wd), lambda bb, i: (bb, i, 0))
    in_specs, args = [], []
    if conv:
        prev, nxt = _halo_specs(tm, t, d)
        in_specs += [prev, tile(d), nxt]
        args += [x, x, x]
    else:
        in_specs += [tile(d)]
        args += [x]
    in_specs += [_mod_spec(row, 0, d, per_batch), _mod_spec(row, 1, d, per_batch),
                 _layer_spec(g_pre, l), _layer_spec(w, l, single_buffer=True)]
    args += [mod, mod, g_pre, w]
    if rope:
        in_specs += [pl.BlockSpec((tm, V7X_LANES), lambda bb, i: (i, 0))] * 2
        args += list(tables)
    out_shape = [jax.ShapeDtypeStruct((b, t, wd), BF16) for wd in widths]
    out_specs = [tile(wd) for wd in widths]
    scratch = [pltpu.VMEM((tm, d), BF16)]
    conf_k = cw = r_col = 0
    if conv:
        w_conf_dw, b_conf_dw, ln_g, ln_b, w_sc_dw, g_group3 = conv_params
        conf_k, cw = w_conf_dw.shape[1:]
        r_col = n - 5 * cw
        assert w_sc_dw.shape[2] == cw and conf_k // 2 < HALO and g_group3.shape[2] == ATTN_W + 2 * cw
        in_specs += [_layer_spec(w_conf_dw, l), _layer_spec(b_conf_dw, l), _layer_spec(ln_g, l),
                     _layer_spec(ln_b, l), _layer_spec(w_sc_dw, l),
                     _layer_spec(g_group3, l, cols=cw, col_block=ATTN_W // cw),
                     _layer_spec(g_group3, l, cols=cw, col_block=ATTN_W // cw + 1)]
        args += [w_conf_dw, b_conf_dw, ln_g, ln_b, w_sc_dw, g_group3, g_group3]
        out_shape.append(jax.ShapeDtypeStruct((b, t, 2 * cw), BF16))
        out_specs.append(tile(2 * cw))
        rows = tm + 2 * HALO
        scratch = [pltpu.VMEM((rows, d), BF16), pltpu.VMEM((rows, cw), F32), pltpu.VMEM((rows, cw), F32)]
    kern = functools.partial(_mixer_in_kernel, outs=tuple(outs), rope=rope, conv=conv, r_col=r_col,
                             tm=tm, conf_k=conf_k, cw=cw)
    return pl.pallas_call(
        kern,
        out_shape=out_shape,
        grid=(b, t // tm),
        in_specs=in_specs,
        out_specs=out_specs,
        scratch_shapes=scratch,
        compiler_params=_params("parallel", "parallel"),
        name="mixer_in",
    )(*args)


def _mix_kernel(*refs, band, t, tq, sink_base):
    if band:
        (sink_ref, q_ref, kk_ref, vlo_ref, vhi_ref, kc_ref, vclo_ref, vchi_ref, zcs_ref, x_ref,
         gt_ref, sc2_ref, sh2_ref, ga_ref, gpost_ref, gffn_ref, wout_ref, xo_ref, h2_ref) = refs
    else:
        (sink_ref, q_ref, kc_ref, vclo_ref, vchi_ref, zcs_ref, x_ref,
         gt_ref, sc2_ref, sh2_ref, ga_ref, gpost_ref, gffn_ref, wout_ref, xo_ref, h2_ref) = refs
    blk = ATTN_BLOCK
    band_w = blk + 2 * WINDOW
    lane = lax.broadcasted_iota(jnp.int32, (1, V7X_LANES), 1)
    lo_lanes = lane < HEAD_DIM
    m_lo = jnp.where(lo_lanes, 1.0, 0.0).astype(BF16)
    m_hi = jnp.where(lo_lanes, 0.0, 1.0).astype(BF16)
    t0 = pl.program_id(1) * tq
    n_blocks = tq // blk
    groups = [(qb, h) for qb in range(n_blocks) for h in range(N_KV_HEADS)]

    def window(qb):
        q0 = t0 + qb * blk
        return q0, pl.multiple_of(jnp.clip(q0 - WINDOW, 0, t - band_w), blk)

    def scores(qb, h):
        rows = slice(qb * blk, (qb + 1) * blk)
        kv_lanes = slice(h * V7X_LANES, (h + 1) * V7X_LANES)
        qp0 = q_ref[0, rows, (2 * h) * V7X_LANES:(2 * h + 1) * V7X_LANES]
        qp1 = q_ref[0, rows, (2 * h + 1) * V7X_LANES:(2 * h + 2) * V7X_LANES]
        qs = jnp.concatenate([qp0 * m_lo, qp1 * m_lo, qp0 * m_hi, qp1 * m_hi], axis=0)
        s_ctx = _dot_nt(qs, kc_ref[0, :, kv_lanes])
        s_band = None
        if band:
            _, start = window(qb)
            s_band = _dot_nt(qs, kk_ref[0, pl.ds(start, band_w), kv_lanes])
        return s_ctx, s_band

    visible = {}

    def attend(qb, h, s_ctx, s_band):
        kv_lanes = slice(h * V7X_LANES, (h + 1) * V7X_LANES)
        heads = (4 * h, 4 * h + 2, 4 * h + 1, 4 * h + 3)
        if band:
            q0, start = window(qb)
            if qb not in visible:
                rel = (lax.broadcasted_iota(jnp.int32, (blk, band_w), 1)
                       - lax.broadcasted_iota(jnp.int32, (blk, band_w), 0)) + (start - q0)
                visible[qb] = jnp.abs(rel) <= WINDOW
        p_ctx, p_band, snk = [], [], []
        for r, head in enumerate(heads):
            rr = slice(r * blk, (r + 1) * blk)
            sink2 = sink_ref[sink_base + head] * LOG2E
            sc = s_ctx[rr]
            m = jnp.maximum(jnp.max(sc, axis=-1, keepdims=True), sink2)
            if band:
                sb = jnp.where(visible[qb], s_band[rr], NEG)
                m = jnp.maximum(m, jnp.max(sb, axis=-1, keepdims=True))
                p_band.append(jnp.exp2(sb - m).astype(BF16))
            p_ctx.append(jnp.exp2(sc - m).astype(BF16))
            snk.append(jnp.exp2(sink2 - m))
        o_lo = _dot(jnp.concatenate(p_ctx[0:2], axis=0), vclo_ref[0, :, kv_lanes])
        o_hi = _dot(jnp.concatenate(p_ctx[2:4], axis=0), vchi_ref[0, :, kv_lanes])
        if band:
            o_lo = o_lo + _dot(jnp.concatenate(p_band[0:2], axis=0),
                               vlo_ref[0, pl.ds(start, band_w), kv_lanes])
            o_hi = o_hi + _dot(jnp.concatenate(p_band[2:4], axis=0),
                               vhi_ref[0, pl.ds(start, band_w), kv_lanes])
        num = jnp.where(lo_lanes, o_lo, o_hi)
        den = jnp.where(lo_lanes,
                        pltpu.roll(o_lo, HEAD_DIM, 1) + jnp.concatenate(snk[0:2], axis=0),
                        pltpu.roll(o_hi, HEAD_DIM, 1) + jnp.concatenate(snk[2:4], axis=0))
        o = num / den
        return [o[0:blk], o[blk:2 * blk]]

    merge_rows = min(tq, MIX_MERGE_ROWS)
    blocks_per_merge = merge_rows // blk
    post_gain = gt_ref[0] * gpost_ref[0]
    ffn_gain = gffn_ref[0] * (1.0 + sc2_ref[0])

    def merge(mi, tiles):
        rows = slice(mi * merge_rows, (mi + 1) * merge_rows)
        a = jnp.concatenate(
            [jnp.concatenate([tl for h in range(N_KV_HEADS) for tl in tiles[(qb, h)]], axis=-1)
             for qb in range(mi * blocks_per_merge, (mi + 1) * blocks_per_merge)], axis=0)
        za = (_rms(a) * ga_ref[0]).astype(BF16)
        y = _dot(za, wout_ref[0, 0:ATTN_W, :]) + _dot(zcs_ref[0, rows, :], wout_ref[0, ATTN_W:, :])
        x_mid = x_ref[0, rows, :] + _rms(y) * post_gain
        xo_ref[0, rows, :] = x_mid
        h2_ref[0, rows, :] = (_rms(x_mid) * ffn_gain + sh2_ref[0]).astype(BF16)

    tiles = {}
    nxt = scores(*groups[0])
    for gi, (qb, h) in enumerate(groups):
        cur = nxt
        if gi + 1 < len(groups):
            nxt = scores(*groups[gi + 1])
        tiles[(qb, h)] = attend(qb, h, *cur)
        if h == N_KV_HEADS - 1 and (qb + 1) % blocks_per_merge == 0:
            merge(qb // blocks_per_merge, tiles)


def _mix(sink, l, q, kk, vlo, vhi, kc, vclo, vchi, zcs, x, mod, row, per_batch,
         g_group3, g_post, g_ffn, w_out, tq):
    b, t, d = x.shape
    band = kk is not None
    ctx_len = kc.shape[1]
    assert t % tq == 0 and tq % min(tq, MIX_MERGE_ROWS) == 0 and MIX_MERGE_ROWS % ATTN_BLOCK == 0
    tile = lambda w: pl.BlockSpec((1, tq, w), lambda bb, i: (bb, i, 0))
    whole = lambda n, w: pl.BlockSpec((1, n, w), lambda bb, i: (bb, 0, 0))
    in_specs = [pl.BlockSpec(memory_space=pltpu.SMEM), tile(ATTN_W)]
    args = [sink, q]
    if band:
        in_specs += [whole(t, KV_DUP_W)] * 3
        args += [kk, vlo, vhi]
    in_specs += [whole(ctx_len, KV_DUP_W)] * 3
    in_specs += [tile(zcs.shape[2]), tile(d),
                 _mod_spec(row, 2, d, per_batch), _mod_spec(row, 4, d, per_batch),
                 _mod_spec(row, 3, d, per_batch),
                 _layer_spec(g_group3, l, cols=ATTN_W), _layer_spec(g_post, l), _layer_spec(g_ffn, l),
                 _layer_spec(w_out, l)]
    args += [kc, vclo, vchi, zcs, x, mod, mod, mod, g_group3, g_post, g_ffn, w_out]
    kern = functools.partial(_mix_kernel, band=band, t=t, tq=tq, sink_base=l * N_Q_HEADS)
    return pl.pallas_call(
        kern,
        out_shape=[jax.ShapeDtypeStruct((b, t, d), F32), jax.ShapeDtypeStruct((b, t, d), BF16)],
        grid=(b, t // tq),
        in_specs=in_specs,
        out_specs=[tile(d), tile(d)],
        compiler_params=_params("parallel", "parallel"),
        name="mix",
    )(*args)


def _ffn_kernel(hp_ref, h_ref, hn_ref, x_ref, gt_ref, gpost_ref, wup_ref, wdw_ref, wdn_ref, o_ref,
                hcat_ref, act_ref, *, tm, d_ff, tf):
    i = pl.program_id(1)
    rows = tm + 2 * HALO
    hcat_ref[HALO:HALO + tm, :] = h_ref[0]

    @pl.when(i > 0)
    def _():
        hcat_ref[0:HALO, :] = hp_ref[0]

    @pl.when(i == 0)
    def _():
        hcat_ref[0:HALO, :] = jnp.zeros((HALO, hcat_ref.shape[1]), BF16)

    @pl.when(i < pl.num_programs(1) - 1)
    def _():
        hcat_ref[HALO + tm:, :] = hn_ref[0]

    @pl.when(i == pl.num_programs(1) - 1)
    def _():
        hcat_ref[HALO + tm:, :] = jnp.zeros((HALO, hcat_ref.shape[1]), BF16)

    hc = hcat_ref[...]
    mid = slice(HALO, HALO + tm)

    def conv3(u, w):
        return (pltpu.roll(u, 1, 0)[mid] * w[0:1, :] + u[mid] * w[1:2, :]
                + pltpu.roll(u, rows - 1, 0)[mid] * w[2:3, :])

    n_chunks = d_ff // tf

    def up(c):
        gcols = slice(c * tf, (c + 1) * tf)
        vcols = slice(d_ff + c * tf, d_ff + (c + 1) * tf)
        return _dot(hc, wup_ref[0, :, gcols]), _dot(hc, wup_ref[0, :, vcols])

    nxt = up(0)
    for c in range(n_chunks):
        ug, uv = nxt
        if c + 1 < n_chunks:
            nxt = up(c + 1)
        gcols = slice(c * tf, (c + 1) * tf)
        vcols = slice(d_ff + c * tf, d_ff + (c + 1) * tf)
        gate = conv3(ug, wdw_ref[0, :, gcols])
        val = conv3(uv, wdw_ref[0, :, vcols])
        act_ref[:, gcols] = (gate * jax.nn.sigmoid(gate) * val).astype(BF16)
    y = _dot(act_ref[...], wdn_ref[0])
    o_ref[0] = x_ref[0] + (gt_ref[0] * gpost_ref[0]) * _rms(y)


def _ffn(h2, x, mod, row, per_batch, l, g_post, w_up, w_dw, w_down, tm):
    b, t, d = x.shape
    d_ff = w_down.shape[1]
    assert t % tm == 0 and tm % HALO == 0 and d_ff % FFN_TF == 0
    prev, nxt = _halo_specs(tm, t, d)
    tile = pl.BlockSpec((1, tm, d), lambda bb, i: (bb, i, 0))
    kern = functools.partial(_ffn_kernel, tm=tm, d_ff=d_ff, tf=FFN_TF)
    return pl.pallas_call(
        kern,
        out_shape=jax.ShapeDtypeStruct((b, t, d), F32),
        grid=(b, t // tm),
        in_specs=[prev, tile, nxt, tile, _mod_spec(row, 5, d, per_batch), _layer_spec(g_post, l),
                  _layer_spec(w_up, l, single_buffer=True), _layer_spec(w_dw, l),
                  _layer_spec(w_down, l, single_buffer=True)],
        out_specs=tile,
        scratch_shapes=[pltpu.VMEM((tm + 2 * HALO, d), BF16), pltpu.VMEM((tm, d_ff), BF16)],
        compiler_params=_params("parallel", "parallel"),
        name="conv_ffn",
    )(h2, h2, h2, x, mod, g_post, w_up, w_dw, w_down)


def _rope_tables(n_tokens):
    lane = np.arange(V7X_LANES)
    row_axis = ((lane % HEAD_DIM) // (2 * ROPE_FREQS)) == 0
    sign = np.where((lane % (2 * ROPE_FREQS)) < ROPE_FREQS, -1.0, 1.0).astype(np.float32)
    inv = ROPE_THETA ** (-jnp.arange(ROPE_FREQS, dtype=F32) / ROPE_FREQS)
    tok = jnp.arange(n_tokens)
    pos = jnp.where(jnp.asarray(row_axis)[None, :], (tok // GRID_W)[:, None], (tok % GRID_W)[:, None])
    ang = pos.astype(F32) * jnp.tile(inv, V7X_LANES // ROPE_FREQS)[None, :]
    return jnp.cos(ang), jnp.sin(ang) * sign[None, :]


def _augment_w_in(w_in):
    parts = [w_in[:, :, :ATTN_W]]
    for base in (ATTN_W, ATTN_W + KV_W):
        for h in range(N_KV_HEADS):
            head = w_in[:, :, base + h * HEAD_DIM:base + (h + 1) * HEAD_DIM]
            parts += [head, head]
    parts.append(w_in[:, :, ATTN_W + 2 * KV_W:])
    return jnp.concatenate(parts, axis=-1).astype(BF16)


def kernel(x, c, ctx, c_ctx, w_ada, b_ada, g_pre_mix, g_post_mix, g_pre_ffn, g_post_ffn, w_in, sink,
           w_conf_dw, b_conf_dw, conf_ln_g, conf_ln_b, w_sc_dw, g_group, w_out, w_up, w_ffn_dw, w_down):
    batch, seq, d = x.shape
    depth = w_in.shape[0]
    ctx_len = ctx.shape[1]
    assert batch < MOD_ROWS

    cc = jnp.concatenate([c, c_ctx[None, :], jnp.zeros((MOD_ROWS - batch - 1, d), F32)], axis=0)
    mod = _modulation(cc, w_ada, b_ada).reshape(depth * MOD_ROWS, 1, N_MOD * d)
    tables = _rope_tables(seq)
    sink_flat = sink.reshape(-1).astype(F32)
    row3 = lambda a: a.reshape(depth, 1, a.shape[-1])
    w_in_b, w_out_b, w_up_b, w_down_b = (_augment_w_in(w_in), w_out.astype(BF16), w_up.astype(BF16),
                                          w_down.astype(BF16))
    g_pre_mix, g_post_mix, g_pre_ffn, g_post_ffn, g_group3 = map(
        row3, (g_pre_mix, g_post_mix, g_pre_ffn, g_post_ffn, g_group))
    conv_params = (w_conf_dw, row3(b_conf_dw), row3(conf_ln_g), row3(conf_ln_b), w_sc_dw, g_group3)

    k_col, v_col = ATTN_W, ATTN_W + KV_DUP_W
    kv_outs = [(k_col, KV_DUP_W, "k"), (v_col, KV_DUP_W, "v_lo"), (v_col, KV_DUP_W, "v_hi")]
    all_outs = [(0, ATTN_W, "q")] + kv_outs

    for l in range(depth):
        update_ctx = l < depth - 1
        row = l * MOD_ROWS
        q, kk, vlo, vhi, zcs = _mixer_in(x, mod, row, True, g_pre_mix, w_in_b, l, tables, all_outs,
                                         conv_params, IN_TM)
        if update_ctx:
            qc, kc, vclo, vchi, zcs_c = _mixer_in(ctx, mod, row + batch, False, g_pre_mix, w_in_b, l, None,
                                                  all_outs, conv_params, ctx_len)
        else:
            kc, vclo, vchi = _mixer_in(ctx, mod, row + batch, False, g_pre_mix, w_in_b, l, None,
                                       kv_outs, None, ctx_len)
        x_mid, h2 = _mix(sink_flat, l, q, kk, vlo, vhi, kc, vclo, vchi, zcs, x, mod, row, True,
                         g_group3, g_post_mix, g_pre_ffn, w_out_b, MIX_TQ)
        x = _ffn(h2, x_mid, mod, row, True, l, g_post_ffn, w_up_b, w_ffn_dw, w_down_b, FFN_TM)
        if update_ctx:
            ctx_mid, hc2 = _mix(sink_flat, l, qc, None, None, None, kc, vclo, vchi, zcs_c, ctx, mod,
                                row + batch, False, g_group3, g_post_mix, g_pre_ffn, w_out_b, ctx_len)
            ctx = _ffn(hc2, ctx_mid, mod, row + batch, False, l, g_post_ffn, w_up_b, w_ffn_dw, w_down_b,
                       ctx_len)
    return x
```

```python
import functools

import numpy as np
import jax
import jax.numpy as jnp
from jax import lax
from jax.experimental import pallas as pl
from jax.experimental.pallas import tpu as pltpu

F32 = jnp.float32
BF16 = jnp.bfloat16

V7X_LANES = 128
V7X_SUBLANES = 8
V7X_BF16_SUBLANES = 16
V7X_VMEM_LIMIT_BYTES = 56 * 1024 * 1024

HEAD_DIM = 64
N_Q_HEADS = 8
N_KV_HEADS = 2
GRID_W = 64
WINDOW = 128
ATTN_BLOCK = 128
ROPE_FREQS = HEAD_DIM // 4
ROPE_THETA = 10000.0
EPS = 1e-6
NEG = -1e30
LOG2E = 1.4426950408889634

ATTN_W = N_Q_HEADS * HEAD_DIM
KV_W = N_KV_HEADS * HEAD_DIM
KV_DUP_W = 2 * KV_W
N_MOD = 6
MOD_ROWS = 16

IN_TM = 512
MIX_TQ = 512
MIX_MERGE_ROWS = 256
FFN_TM = 512
FFN_TF = 256
HALO = V7X_BF16_SUBLANES
CONV_RC = 64
PROJ_CHUNK = 2 * V7X_LANES
MOD_TN = 1536


def _rms(x, eps=EPS):
    return x * lax.rsqrt(jnp.mean(x * x, axis=-1, keepdims=True) + eps)


def _dot(a, b):
    return jnp.dot(a, b, preferred_element_type=F32)


def _dot_nt(a, b):
    return lax.dot_general(a, b, (((1,), (1,)), ((), ())), preferred_element_type=F32)


def _params(*sem):
    return pltpu.CompilerParams(dimension_semantics=sem, vmem_limit_bytes=V7X_VMEM_LIMIT_BYTES)


def _layer_spec(arr, l, cols=None, col_block=0, single_buffer=False):
    _, rows, width = arr.shape
    width = cols or width
    mode = dict(pipeline_mode=pl.Buffered(1)) if single_buffer else {}
    return pl.BlockSpec((1, rows, width), lambda *_: (l, 0, col_block), **mode)


def _rows_spec(arr, cols=None, col_block=0):
    depth, width = arr.shape
    return pl.BlockSpec((depth, cols or width), lambda *_: (0, col_block))


def _mod_spec(row, k, d, per_batch):
    if per_batch:
        return pl.BlockSpec((1, 1, d), lambda b, i: (row + b, 0, k))
    return pl.BlockSpec((1, 1, d), lambda b, i: (row, 0, k))


def _halo_specs(tm, t, d):
    per_tile = tm // HALO
    last = t // HALO - 1
    return (pl.BlockSpec((1, HALO, d), lambda bb, i: (bb, jnp.maximum(i * per_tile - 1, 0), 0)),
            pl.BlockSpec((1, HALO, d), lambda bb, i: (bb, jnp.minimum((i + 1) * per_tile, last), 0)))


def _mod_kernel(c_ref, w_ref, b_ref, o_ref):
    c = c_ref[...]
    a = c * jax.nn.sigmoid(c)
    w = w_ref[0]
    a_hi = a.astype(BF16)
    a_lo = (a - a_hi.astype(F32)).astype(BF16)
    w_hi = w.astype(BF16)
    w_lo = (w - w_hi.astype(F32)).astype(BF16)
    acc = _dot(a_hi, w_hi) + _dot(a_lo, w_hi) + _dot(a_hi, w_lo)
    o_ref[0] = acc + b_ref[0]


def _modulation(cc, w_ada, b_ada):
    depth, d, n = w_ada.shape
    return pl.pallas_call(
        _mod_kernel,
        out_shape=jax.ShapeDtypeStruct((depth, MOD_ROWS, n), F32),
        grid=(depth, n // MOD_TN),
        in_specs=[
            pl.BlockSpec((MOD_ROWS, d), lambda l, j: (0, 0)),
            pl.BlockSpec((1, d, MOD_TN), lambda l, j: (l, 0, j)),
            pl.BlockSpec((1, 1, MOD_TN), lambda l, j: (l, 0, j)),
        ],
        out_specs=pl.BlockSpec((1, MOD_ROWS, MOD_TN), lambda l, j: (l, 0, j)),
        compiler_params=_params("parallel", "parallel"),
        name="modulation",
    )(cc, w_ada, b_ada.reshape(depth, 1, n))


def _project(h_ref, rows, w_refs, out_refs, outs, rope, cos, sin):
    lane = lax.broadcasted_iota(jnp.int32, (1, V7X_LANES), 1)
    lo_lanes = lane < HEAD_DIM
    first_half = (lane % (2 * ROPE_FREQS)) < ROPE_FREQS
    chunks = {}
    for o_ref, (wi, col, width, kind) in zip(out_refs, outs):
        for c0 in range(0, width, PROJ_CHUNK):
            chunks.setdefault((wi, col + c0, col + c0 + PROJ_CHUNK), []).append((o_ref, c0, kind))
    prev = None
    for cols, users in list(chunks.items()) + [(None, None)]:
        cur = (_dot(h_ref[rows, :], w_refs[cols[0]][0, :, cols[1]:cols[2]]), users) if users else None
        if prev is not None:
            p, p_users = prev
            for o_ref, c0, kind in p_users:
                tiles = []
                for j in range(PROJ_CHUNK // V7X_LANES):
                    t = p[:, j * V7X_LANES:(j + 1) * V7X_LANES]
                    if rope and kind in ("q", "k"):
                        partner = jnp.where(first_half,
                                            pltpu.roll(t, V7X_LANES - ROPE_FREQS, 1),
                                            pltpu.roll(t, ROPE_FREQS, 1))
                        t = t * cos + partner * sin
                    if kind == "q":
                        t = t * (HEAD_DIM ** -0.5 * LOG2E)
                    elif kind == "v_lo":
                        t = jnp.where(lo_lanes, t, 1.0)
                    elif kind == "v_hi":
                        t = jnp.where(lo_lanes, 1.0, t)
                    tiles.append(t.astype(BF16))
                o_ref[0, :, c0:c0 + PROJ_CHUNK] = jnp.concatenate(tiles, axis=-1)
        prev = cur
        yield None if cur is None else cur[0][0:V7X_SUBLANES, 0:V7X_LANES]


def _mixer_in_kernel(*refs, l, outs, rope, conv, r_col, tm, conf_k, cw):
    it = iter(refs)
    if conv:
        xp_ref, x_ref, xn_ref = next(it), next(it), next(it)
    else:
        x_ref = next(it)
    sh_ref, sc_ref, g_ref, w_ref, wkv_ref = next(it), next(it), next(it), next(it), next(it)
    w_refs = (w_ref, wkv_ref)
    layer = slice(l, l + 1)
    cos = sin = None
    if rope:
        cos, sin = next(it)[...], next(it)[...]
    if conv:
        wcf_ref, bcf_ref, lng_ref, lnb_ref, wsc_ref, gc_ref, gs_ref = (next(it) for _ in range(7))
    out_refs = [next(it) for _ in outs]
    if conv:
        zcs_ref, h_ref, u_ref, v_ref = next(it), next(it), next(it), next(it)
    else:
        h_ref = next(it)

    gain = g_ref[layer, :] * (1.0 + sc_ref[0])
    shift = sh_ref[0]

    def modulate(xv):
        return (_rms(xv) * gain + shift).astype(BF16)

    if not conv:
        h_ref[...] = modulate(x_ref[0])
        for _ in _project(h_ref, slice(0, tm), w_refs, out_refs, outs, rope, cos, sin):
            pass
        return

    i = pl.program_id(1)
    top_ok = (i > 0).astype(F32)
    bot_ok = (i < pl.num_programs(1) - 1).astype(F32)
    mid = slice(HALO, HALO + tm)
    top = slice(0, HALO)
    bot = slice(HALO + tm, HALO + tm + HALO)
    h_ref[mid, :] = modulate(x_ref[0])
    h_ref[top, :] = modulate(xp_ref[0])
    h_ref[bot, :] = modulate(xn_ref[0])

    def rdot(k):
        return _dot(h_ref[...], w_ref[0, :, r_col + k * cw:r_col + (k + 1) * cw])

    def store_padded(ref, val):
        ref[mid, :] = val[mid]
        ref[top, :] = val[top] * top_ok
        ref[bot, :] = val[bot] * bot_ok

    cv, cg = rdot(0), rdot(1)
    store_padded(u_ref, cv * jax.nn.sigmoid(cg))

    rc = min(CONV_RC, tm)
    win = rc + 2 * HALO
    pad_c = conf_k // 2

    def conformer_chunk(c, after):
        r0 = c * rc
        zero = None
        if after is not None:
            half = jnp.uint32(16)
            bits = lax.shift_right_logical(lax.shift_right_logical(pltpu.bitcast(after, jnp.uint32), half), half)
            zero = pltpu.bitcast(bits, F32)[0:1, 0:1]
        halves = []
        for hh in range(cw // V7X_LANES):
            lanes = slice(hh * V7X_LANES, (hh + 1) * V7X_LANES)
            wdw = wcf_ref[0, :, lanes]
            window = u_ref[r0:r0 + win, lanes]
            acc = None
            for res in range(V7X_SUBLANES):
                shifted = window if res == 0 else pltpu.roll(window, win - res, 0)
                for k in range(conf_k):
                    off = HALO - pad_c + k
                    if off % V7X_SUBLANES != res:
                        continue
                    term = shifted[off - res:off - res + rc, :] * wdw[k:k + 1, :]
                    acc = term if acc is None else acc + term
            halves.append(acc)
        bias = bcf_ref[layer, :] if zero is None else bcf_ref[layer, :] + zero
        y = jnp.concatenate(halves, axis=-1) + bias
        cen = y - jnp.mean(y, axis=-1, keepdims=True)
        var = jnp.mean(cen * cen, axis=-1, keepdims=True)
        ln = cen * lax.rsqrt(var + EPS) * lng_ref[layer, :] + lnb_ref[layer, :]
        cf = ln * jax.nn.sigmoid(ln)
        zcs_ref[0, r0:r0 + rc, 0:cw] = (_rms(cf) * gc_ref[layer, :]).astype(BF16)

    def short_conv(sb):
        vwin = v_ref[...]
        wsc = wsc_ref[0]
        s = sb[mid] * (pltpu.roll(vwin, 1, 0)[mid] * wsc[0:1, :] + vwin[mid] * wsc[1:2, :]
                       + pltpu.roll(vwin, tm + 2 * HALO - 1, 0)[mid] * wsc[2:3, :])
        zcs_ref[0, :, cw:2 * cw] = (_rms(s) * gs_ref[layer, :]).astype(BF16)

    piece = lambda val: val[0:V7X_SUBLANES, 0:V7X_LANES]
    short_in = {}

    def mxu_work():
        short_in["sb"] = rdot(2)
        yield piece(short_in["sb"])
        scg = rdot(3)
        yield piece(scg)
        su = rdot(4)
        store_padded(v_ref, scg * su)
        yield piece(su)
        yield from _project(h_ref, mid, w_refs, out_refs, outs, rope, cos, sin)

    work = mxu_work()
    pending = None
    for c in range(tm // rc):
        after, pending = pending, next(work, None)
        conformer_chunk(c, after)
    for _ in work:
        pass
    short_conv(short_in["sb"])


def _mixer_in(x, mod, row, per_batch, g_pre, w, w_kv, l, tables, outs, conv_params, tm):
    b, t, d = x.shape
    n = w.shape[2]
    rope = tables is not None
    conv = conv_params is not None
    widths = [o[2] for o in outs]
    assert t % tm == 0 and tm % HALO == 0 and all(o[1] + o[2] <= (w, w_kv)[o[0]].shape[2] for o in outs)
    tile = lambda wd: pl.BlockSpec((1, tm, wd), lambda bb, i: (bb, i, 0))
    in_specs, args = [], []
    if conv:
        prev, nxt = _halo_specs(tm, t, d)
        in_specs += [prev, tile(d), nxt]
        args += [x, x, x]
    else:
        in_specs += [tile(d)]
        args += [x]
    in_specs += [_mod_spec(row, 0, d, per_batch), _mod_spec(row, 1, d, per_batch),
                 _rows_spec(g_pre), _layer_spec(w, l, single_buffer=True),
                 _layer_spec(w_kv, l, single_buffer=True)]
    args += [mod, mod, g_pre, w, w_kv]
    if rope:
        in_specs += [pl.BlockSpec((tm, V7X_LANES), lambda bb, i: (i, 0))] * 2
        args += list(tables)
    out_shape = [jax.ShapeDtypeStruct((b, t, wd), BF16) for wd in widths]
    out_specs = [tile(wd) for wd in widths]
    scratch = [pltpu.VMEM((tm, d), BF16)]
    conf_k = cw = r_col = 0
    if conv:
        w_conf_dw, b_conf_dw, ln_g, ln_b, w_sc_dw, g_group = conv_params
        conf_k, cw = w_conf_dw.shape[1:]
        r_col = n - 5 * cw
        assert w_sc_dw.shape[2] == cw and conf_k // 2 < HALO and g_group.shape[1] == ATTN_W + 2 * cw
        in_specs += [_layer_spec(w_conf_dw, l), _rows_spec(b_conf_dw), _rows_spec(ln_g), _rows_spec(ln_b),
                     _layer_spec(w_sc_dw, l),
                     _rows_spec(g_group, cols=cw, col_block=ATTN_W // cw),
                     _rows_spec(g_group, cols=cw, col_block=ATTN_W // cw + 1)]
        args += [w_conf_dw, b_conf_dw, ln_g, ln_b, w_sc_dw, g_group, g_group]
        out_shape.append(jax.ShapeDtypeStruct((b, t, 2 * cw), BF16))
        out_specs.append(tile(2 * cw))
        rows = tm + 2 * HALO
        scratch = [pltpu.VMEM((rows, d), BF16), pltpu.VMEM((rows, cw), F32), pltpu.VMEM((rows, cw), F32)]
    kern = functools.partial(_mixer_in_kernel, l=l, outs=tuple(outs), rope=rope, conv=conv, r_col=r_col,
                             tm=tm, conf_k=conf_k, cw=cw)
    return pl.pallas_call(
        kern,
        out_shape=out_shape,
        grid=(b, t // tm),
        in_specs=in_specs,
        out_specs=out_specs,
        scratch_shapes=scratch,
        compiler_params=_params("parallel", "parallel"),
        name="mixer_in",
    )(*args)


def _mix_kernel(*refs, band, t, tq, l):
    if band:
        (sink_ref, q_ref, kk_ref, vlo_ref, vhi_ref, kc_ref, vclo_ref, vchi_ref, zcs_ref, x_ref,
         gt_ref, sc2_ref, sh2_ref, ga_ref, gpost_ref, gffn_ref, wout_ref, xo_ref, h2_ref) = refs
    else:
        (sink_ref, q_ref, kc_ref, vclo_ref, vchi_ref, zcs_ref, x_ref,
         gt_ref, sc2_ref, sh2_ref, ga_ref, gpost_ref, gffn_ref, wout_ref, xo_ref, h2_ref) = refs
    blk = ATTN_BLOCK
    band_w = blk + 2 * WINDOW
    lane = lax.broadcasted_iota(jnp.int32, (1, V7X_LANES), 1)
    lo_lanes = lane < HEAD_DIM
    m_lo = jnp.where(lo_lanes, 1.0, 0.0).astype(BF16)
    m_hi = jnp.where(lo_lanes, 0.0, 1.0).astype(BF16)
    t0 = pl.program_id(1) * tq
    n_blocks = tq // blk
    groups = [(qb, h) for qb in range(n_blocks) for h in range(N_KV_HEADS)]

    def window(qb):
        q0 = t0 + qb * blk
        return q0, pl.multiple_of(jnp.clip(q0 - WINDOW, 0, t - band_w), blk)

    def scores(qb, h):
        rows = slice(qb * blk, (qb + 1) * blk)
        kv_lanes = slice(h * V7X_LANES, (h + 1) * V7X_LANES)
        qp0 = q_ref[0, rows, (2 * h) * V7X_LANES:(2 * h + 1) * V7X_LANES]
        qp1 = q_ref[0, rows, (2 * h + 1) * V7X_LANES:(2 * h + 2) * V7X_LANES]
        qs = jnp.concatenate([qp0 * m_lo, qp1 * m_lo, qp0 * m_hi, qp1 * m_hi], axis=0)
        s_ctx = _dot_nt(qs, kc_ref[0, :, kv_lanes])
        s_band = None
        if band:
            _, start = window(qb)
            s_band = _dot_nt(qs, kk_ref[0, pl.ds(start, band_w), kv_lanes])
        return s_ctx, s_band

    hidden = {}

    def attend(qb, h, s_ctx, s_band):
        kv_lanes = slice(h * V7X_LANES, (h + 1) * V7X_LANES)
        heads = (4 * h, 4 * h + 2, 4 * h + 1, 4 * h + 3)
        if band:
            q0, start = window(qb)
            if qb not in hidden:
                rel = (lax.broadcasted_iota(jnp.int32, (blk, band_w), 1)
                       - lax.broadcasted_iota(jnp.int32, (blk, band_w), 0)) + (start - q0)
                hidden[qb] = jnp.where(jnp.abs(rel) <= WINDOW, 0.0, NEG)
        p_ctx, p_band, snk = [], [], []
        for r, head in enumerate(heads):
            rr = slice(r * blk, (r + 1) * blk)
            sink2 = sink_ref[l * N_Q_HEADS + head] * LOG2E
            sc = s_ctx[rr]
            m = jnp.maximum(jnp.max(sc, axis=-1, keepdims=True), sink2)
            if band:
                sb = s_band[rr] + hidden[qb]
                m = jnp.maximum(m, jnp.max(sb, axis=-1, keepdims=True))
                p_band.append(jnp.exp2(sb - m).astype(BF16))
            p_ctx.append(jnp.exp2(sc - m).astype(BF16))
            snk.append(jnp.exp2(sink2 - m))
        o_lo = _dot(jnp.concatenate(p_ctx[0:2], axis=0), vclo_ref[0, :, kv_lanes])
        o_hi = _dot(jnp.concatenate(p_ctx[2:4], axis=0), vchi_ref[0, :, kv_lanes])
        if band:
            o_lo = o_lo + _dot(jnp.concatenate(p_band[0:2], axis=0),
                               vlo_ref[0, pl.ds(start, band_w), kv_lanes])
            o_hi = o_hi + _dot(jnp.concatenate(p_band[2:4], axis=0),
                               vhi_ref[0, pl.ds(start, band_w), kv_lanes])
        num = jnp.where(lo_lanes, o_lo, o_hi)
        den = jnp.where(lo_lanes,
                        pltpu.roll(o_lo, HEAD_DIM, 1) + jnp.concatenate(snk[0:2], axis=0),
                        pltpu.roll(o_hi, HEAD_DIM, 1) + jnp.concatenate(snk[2:4], axis=0))
        o = num / den
        return [o[0:blk], o[blk:2 * blk]]

    merge_rows = min(tq, MIX_MERGE_ROWS)
    blocks_per_merge = merge_rows // blk
    layer = slice(l, l + 1)
    post_gain = gt_ref[0] * gpost_ref[layer, :]
    ffn_gain = gffn_ref[layer, :] * (1.0 + sc2_ref[0])

    def merge(mi, tiles):
        rows = slice(mi * merge_rows, (mi + 1) * merge_rows)
        a = jnp.concatenate(
            [jnp.concatenate([tl for h in range(N_KV_HEADS) for tl in tiles[(qb, h)]], axis=-1)
             for qb in range(mi * blocks_per_merge, (mi + 1) * blocks_per_merge)], axis=0)
        za = (_rms(a) * ga_ref[layer, :]).astype(BF16)
        y = _dot(za, wout_ref[0, 0:ATTN_W, :]) + _dot(zcs_ref[0, rows, :], wout_ref[0, ATTN_W:, :])
        x_mid = x_ref[0, rows, :] + _rms(y) * post_gain
        xo_ref[0, rows, :] = x_mid
        h2_ref[0, rows, :] = (_rms(x_mid) * ffn_gain + sh2_ref[0]).astype(BF16)

    tiles = {}
    nxt = scores(*groups[0])
    for gi, (qb, h) in enumerate(groups):
        cur = nxt
        if gi + 1 < len(groups):
            nxt = scores(*groups[gi + 1])
        tiles[(qb, h)] = attend(qb, h, *cur)
        if h == N_KV_HEADS - 1 and (qb + 1) % blocks_per_merge == 0:
            merge(qb // blocks_per_merge, tiles)


def _mix(sink, l, q, kk, vlo, vhi, kc, vclo, vchi, zcs, x, mod, row, per_batch,
         g_group, g_post, g_ffn, w_out, tq):
    b, t, d = x.shape
    band = kk is not None
    ctx_len = kc.shape[1]
    assert t % tq == 0 and tq % min(tq, MIX_MERGE_ROWS) == 0 and MIX_MERGE_ROWS % ATTN_BLOCK == 0
    tile = lambda w: pl.BlockSpec((1, tq, w), lambda bb, i: (bb, i, 0))
    whole = lambda n, w: pl.BlockSpec((1, n, w), lambda bb, i: (bb, 0, 0))
    in_specs = [pl.BlockSpec(memory_space=pltpu.SMEM), tile(ATTN_W)]
    args = [sink, q]
    if band:
        in_specs += [whole(t, KV_DUP_W)] * 3
        args += [kk, vlo, vhi]
    in_specs += [whole(ctx_len, KV_DUP_W)] * 3
    in_specs += [tile(zcs.shape[2]), tile(d),
                 _mod_spec(row, 2, d, per_batch), _mod_spec(row, 4, d, per_batch),
                 _mod_spec(row, 3, d, per_batch),
                 _rows_spec(g_group, cols=ATTN_W), _rows_spec(g_post), _rows_spec(g_ffn),
                 _layer_spec(w_out, l)]
    args += [kc, vclo, vchi, zcs, x, mod, mod, mod, g_group, g_post, g_ffn, w_out]
    kern = functools.partial(_mix_kernel, band=band, t=t, tq=tq, l=l)
    return pl.pallas_call(
        kern,
        out_shape=[jax.ShapeDtypeStruct((b, t, d), F32), jax.ShapeDtypeStruct((b, t, d), BF16)],
        grid=(b, t // tq),
        in_specs=in_specs,
        out_specs=[tile(d), tile(d)],
        compiler_params=_params("parallel", "parallel"),
        name="mix",
    )(*args)


def _ffn_kernel(hp_ref, h_ref, hn_ref, x_ref, gt_ref, gpost_ref, wup_ref, wdw_ref, wdn_ref, o_ref,
                hcat_ref, act_ref, *, l, tm, d_ff, tf):
    i = pl.program_id(1)
    rows = tm + 2 * HALO
    hcat_ref[HALO:HALO + tm, :] = h_ref[0]

    @pl.when(i > 0)
    def _():
        hcat_ref[0:HALO, :] = hp_ref[0]

    @pl.when(i == 0)
    def _():
        hcat_ref[0:HALO, :] = jnp.zeros((HALO, hcat_ref.shape[1]), BF16)

    @pl.when(i < pl.num_programs(1) - 1)
    def _():
        hcat_ref[HALO + tm:, :] = hn_ref[0]

    @pl.when(i == pl.num_programs(1) - 1)
    def _():
        hcat_ref[HALO + tm:, :] = jnp.zeros((HALO, hcat_ref.shape[1]), BF16)

    hc = hcat_ref[...]
    mid = slice(HALO, HALO + tm)

    def conv3(u, w):
        return (pltpu.roll(u, 1, 0)[mid] * w[0:1, :] + u[mid] * w[1:2, :]
                + pltpu.roll(u, rows - 1, 0)[mid] * w[2:3, :])

    n_chunks = d_ff // tf

    def up(c):
        gcols = slice(c * tf, (c + 1) * tf)
        vcols = slice(d_ff + c * tf, d_ff + (c + 1) * tf)
        return _dot(hc, wup_ref[0, :, gcols]), _dot(hc, wup_ref[0, :, vcols])

    nxt = up(0)
    for c in range(n_chunks):
        ug, uv = nxt
        if c + 1 < n_chunks:
            nxt = up(c + 1)
        gcols = slice(c * tf, (c + 1) * tf)
        vcols = slice(d_ff + c * tf, d_ff + (c + 1) * tf)
        gate = conv3(ug, wdw_ref[0, :, gcols])
        val = conv3(uv, wdw_ref[0, :, vcols])
        act_ref[:, gcols] = (gate * jax.nn.sigmoid(gate) * val).astype(BF16)
    y = _dot(act_ref[...], wdn_ref[0])
    o_ref[0] = x_ref[0] + (gt_ref[0] * gpost_ref[l:l + 1, :]) * _rms(y)


def _ffn(h2, x, mod, row, per_batch, l, g_post, w_up, w_dw, w_down, tm):
    b, t, d = x.shape
    d_ff = w_down.shape[1]
    assert t % tm == 0 and tm % HALO == 0 and d_ff % FFN_TF == 0
    prev, nxt = _halo_specs(tm, t, d)
    tile = pl.BlockSpec((1, tm, d), lambda bb, i: (bb, i, 0))
    kern = functools.partial(_ffn_kernel, l=l, tm=tm, d_ff=d_ff, tf=FFN_TF)
    return pl.pallas_call(
        kern,
        out_shape=jax.ShapeDtypeStruct((b, t, d), F32),
        grid=(b, t // tm),
        in_specs=[prev, tile, nxt, tile, _mod_spec(row, 5, d, per_batch), _rows_spec(g_post),
                  _layer_spec(w_up, l, single_buffer=True), _layer_spec(w_dw, l),
                  _layer_spec(w_down, l, single_buffer=True)],
        out_specs=tile,
        scratch_shapes=[pltpu.VMEM((tm + 2 * HALO, d), BF16), pltpu.VMEM((tm, d_ff), BF16)],
        compiler_params=_params("parallel", "parallel"),
        name="conv_ffn",
    )(h2, h2, h2, x, mod, g_post, w_up, w_dw, w_down)


def _rope_tables(n_tokens):
    lane = np.arange(V7X_LANES)
    row_axis = ((lane % HEAD_DIM) // (2 * ROPE_FREQS)) == 0
    sign = np.where((lane % (2 * ROPE_FREQS)) < ROPE_FREQS, -1.0, 1.0).astype(np.float32)
    inv = ROPE_THETA ** (-jnp.arange(ROPE_FREQS, dtype=F32) / ROPE_FREQS)
    tok = jnp.arange(n_tokens)
    pos = jnp.where(jnp.asarray(row_axis)[None, :], (tok // GRID_W)[:, None], (tok % GRID_W)[:, None])
    ang = pos.astype(F32) * jnp.tile(inv, V7X_LANES // ROPE_FREQS)[None, :]
    return jnp.cos(ang), jnp.sin(ang) * sign[None, :]


def _kv_dup_weights(w_in):
    depth, d, _ = w_in.shape
    heads = w_in[:, :, ATTN_W:ATTN_W + 2 * KV_W].astype(BF16).reshape(depth, d, 2 * N_KV_HEADS, 1, HEAD_DIM)
    return jnp.broadcast_to(heads, (depth, d, 2 * N_KV_HEADS, 2, HEAD_DIM)).reshape(depth, d, 2 * KV_DUP_W)


def kernel(x, c, ctx, c_ctx, w_ada, b_ada, g_pre_mix, g_post_mix, g_pre_ffn, g_post_ffn, w_in, sink,
           w_conf_dw, b_conf_dw, conf_ln_g, conf_ln_b, w_sc_dw, g_group, w_out, w_up, w_ffn_dw, w_down):
    batch, seq, d = x.shape
    depth = w_in.shape[0]
    ctx_len = ctx.shape[1]
    assert batch < MOD_ROWS

    cc = jnp.concatenate([c, c_ctx[None, :], jnp.zeros((MOD_ROWS - batch - 1, d), F32)], axis=0)
    mod = _modulation(cc, w_ada, b_ada).reshape(depth * MOD_ROWS, 1, N_MOD * d)
    tables = _rope_tables(seq)
    sink_flat = sink.reshape(-1).astype(F32)
    w_in_b, w_out_b, w_up_b, w_down_b = (w.astype(BF16) for w in (w_in, w_out, w_up, w_down))
    w_kv = _kv_dup_weights(w_in)
    conv_params = (w_conf_dw, b_conf_dw, conf_ln_g, conf_ln_b, w_sc_dw, g_group)

    kv_outs = [(1, 0, KV_DUP_W, "k"), (1, KV_DUP_W, KV_DUP_W, "v_lo"), (1, KV_DUP_W, KV_DUP_W, "v_hi")]
    all_outs = [(0, 0, ATTN_W, "q")] + kv_outs

    for l in range(depth):
        update_ctx = l < depth - 1
        row = l * MOD_ROWS
        q, kk, vlo, vhi, zcs = _mixer_in(x, mod, row, True, g_pre_mix, w_in_b, w_kv, l, tables, all_outs,
                                         conv_params, IN_TM)
        if update_ctx:
            qc, kc, vclo, vchi, zcs_c = _mixer_in(ctx, mod, row + batch, False, g_pre_mix, w_in_b, w_kv, l,
                                                  None, all_outs, conv_params, ctx_len)
        else:
            kc, vclo, vchi = _mixer_in(ctx, mod, row + batch, False, g_pre_mix, w_in_b, w_kv, l, None,
                                       kv_outs, None, ctx_len)
        x_mid, h2 = _mix(sink_flat, l, q, kk, vlo, vhi, kc, vclo, vchi, zcs, x, mod, row, True,
                         g_group, g_post_mix, g_pre_ffn, w_out_b, MIX_TQ)
        x = _ffn(h2, x_mid, mod, row, True, l, g_post_ffn, w_up_b, w_ffn_dw, w_down_b, FFN_TM)
        if update_ctx:
            ctx_mid, hc2 = _mix(sink_flat, l, qc, None, None, None, kc, vclo, vchi, zcs_c, ctx, mod,
                                row + batch, False, g_group, g_post_mix, g_pre_ffn, w_out_b, ctx_len)
            ctx = _ffn(hc2, ctx_mid, mod, row + batch, False, l, g_post_ffn, w_up_b, w_ffn_dw, w_down_b,
                       ctx_len)
    return x
```

```python
import functools

import numpy as np
import jax
import jax.numpy as jnp
from jax import lax
from jax.experimental import pallas as pl
from jax.experimental.pallas import tpu as pltpu

F32 = jnp.float32
BF16 = jnp.bfloat16

V7X_LANES = 128
V7X_SUBLANES = 8
V7X_BF16_SUBLANES = 16
V7X_VMEM_LIMIT_BYTES = 56 * 1024 * 1024

HEAD_DIM = 64
N_Q_HEADS = 8
N_KV_HEADS = 2
GRID_W = 64
WINDOW = 128
ATTN_BLOCK = 128
ROPE_FREQS = HEAD_DIM // 4
ROPE_THETA = 10000.0
EPS = 1e-6
NEG = -1e30
LOG2E = 1.4426950408889634

ATTN_W = N_Q_HEADS * HEAD_DIM
KV_W = N_KV_HEADS * HEAD_DIM
KV_DUP_W = 2 * KV_W
N_MOD = 6
MOD_ROWS = 16

IN_TM = 512
MIX_TQ = 1024
MIX_MERGE_ROWS = 256
FFN_TM = 512
FFN_TF = 256
HALO = V7X_BF16_SUBLANES
CONV_RC = 64
PROJ_CHUNK = 2 * V7X_LANES
MOD_TN = 1536


def _rms(x, eps=EPS):
    return x * lax.rsqrt(jnp.mean(x * x, axis=-1, keepdims=True) + eps)


def _dot(a, b):
    return jnp.dot(a, b, preferred_element_type=F32)


def _dot_nt(a, b):
    return lax.dot_general(a, b, (((1,), (1,)), ((), ())), preferred_element_type=F32)


def _params(*sem):
    return pltpu.CompilerParams(dimension_semantics=sem, vmem_limit_bytes=V7X_VMEM_LIMIT_BYTES)


def _layer_spec(arr, l, cols=None, col_block=0, single_buffer=False):
    _, rows, width = arr.shape
    width = cols or width
    mode = dict(pipeline_mode=pl.Buffered(1)) if single_buffer else {}
    return pl.BlockSpec((1, rows, width), lambda *_: (l, 0, col_block), **mode)


def _rows_spec(arr, cols=None, col_block=0):
    depth, width = arr.shape
    return pl.BlockSpec((depth, cols or width), lambda *_: (0, col_block))


def _mod_spec(row, k, d, per_batch):
    if per_batch:
        return pl.BlockSpec((1, 1, d), lambda b, i: (row + b, 0, k))
    return pl.BlockSpec((1, 1, d), lambda b, i: (row, 0, k))


def _halo_specs(tm, t, d):
    per_tile = tm // HALO
    last = t // HALO - 1
    return (pl.BlockSpec((1, HALO, d), lambda bb, i: (bb, jnp.maximum(i * per_tile - 1, 0), 0)),
            pl.BlockSpec((1, HALO, d), lambda bb, i: (bb, jnp.minimum((i + 1) * per_tile, last), 0)))


def _mod_kernel(c_ref, w_ref, b_ref, o_ref):
    c = c_ref[...]
    a = c * jax.nn.sigmoid(c)
    w = w_ref[0]
    a_hi = a.astype(BF16)
    a_lo = (a - a_hi.astype(F32)).astype(BF16)
    w_hi = w.astype(BF16)
    w_lo = (w - w_hi.astype(F32)).astype(BF16)
    hi_lo = _dot(jnp.concatenate([a_hi, a_lo], axis=0), w_hi)
    acc = hi_lo[0:MOD_ROWS] + hi_lo[MOD_ROWS:2 * MOD_ROWS] + _dot(a_hi, w_lo)
    o_ref[0] = acc + b_ref[0]


def _modulation(cc, w_ada, b_ada):
    depth, d, n = w_ada.shape
    return pl.pallas_call(
        _mod_kernel,
        out_shape=jax.ShapeDtypeStruct((depth, MOD_ROWS, n), F32),
        grid=(depth, n // MOD_TN),
        in_specs=[
            pl.BlockSpec((MOD_ROWS, d), lambda l, j: (0, 0)),
            pl.BlockSpec((1, d, MOD_TN), lambda l, j: (l, 0, j)),
            pl.BlockSpec((1, 1, MOD_TN), lambda l, j: (l, 0, j)),
        ],
        out_specs=pl.BlockSpec((1, MOD_ROWS, MOD_TN), lambda l, j: (l, 0, j)),
        compiler_params=_params("parallel", "parallel"),
        name="modulation",
    )(cc, w_ada, b_ada.reshape(depth, 1, n))


def _project(h_ref, rows, w_refs, out_refs, outs, rope, cos, sin):
    lane = lax.broadcasted_iota(jnp.int32, (1, V7X_LANES), 1)
    dims_first = lax.broadcasted_iota(jnp.int32, (V7X_LANES, 1), 0) < HEAD_DIM
    first_half = (lane % (2 * ROPE_FREQS)) < ROPE_FREQS
    chunks = {}
    for o_ref, (wi, col, width, kind) in zip(out_refs, outs):
        for c0 in range(0, width, PROJ_CHUNK):
            chunks.setdefault((wi, col + c0, col + c0 + PROJ_CHUNK), []).append((o_ref, c0, kind))
    prev = None
    for cols, users in list(chunks.items()) + [(None, None)]:
        cur = (_dot(h_ref[rows, :], w_refs[cols[0]][0, :, cols[1]:cols[2]]), users) if users else None
        if prev is not None:
            p, p_users = prev
            transposed = {}
            for o_ref, c0, kind in p_users:
                tiles = []
                for j in range(PROJ_CHUNK // V7X_LANES):
                    t = p[:, j * V7X_LANES:(j + 1) * V7X_LANES]
                    if kind in ("v_lo", "v_hi"):
                        if j not in transposed:
                            transposed[j] = t.T
                        keep = dims_first if kind == "v_lo" else jnp.logical_not(dims_first)
                        t_t = jnp.where(keep, transposed[j], 1.0)
                        o_ref[0, c0 + j * V7X_LANES:c0 + (j + 1) * V7X_LANES, :] = t_t.astype(BF16)
                        continue
                    if rope:
                        partner = jnp.where(first_half,
                                            pltpu.roll(t, V7X_LANES - ROPE_FREQS, 1),
                                            pltpu.roll(t, ROPE_FREQS, 1))
                        t = t * cos + partner * sin
                    if kind == "q":
                        t = t * (HEAD_DIM ** -0.5 * LOG2E)
                    tiles.append(t.astype(BF16))
                if tiles:
                    o_ref[0, :, c0:c0 + PROJ_CHUNK] = jnp.concatenate(tiles, axis=-1)
        prev = cur
        yield None if cur is None else cur[0][0:V7X_SUBLANES, 0:V7X_LANES]


def _mixer_in_kernel(*refs, l, outs, rope, conv, r_col, tm, conf_k, cw):
    it = iter(refs)
    if conv:
        xp_ref, x_ref, xn_ref = next(it), next(it), next(it)
    else:
        x_ref = next(it)
    sh_ref, sc_ref, g_ref, w_ref, wkv_ref = next(it), next(it), next(it), next(it), next(it)
    w_refs = (w_ref, wkv_ref)
    layer = slice(l, l + 1)
    cos = sin = None
    if rope:
        cos, sin = next(it)[...], next(it)[...]
    if conv:
        wcf_ref, bcf_ref, lng_ref, lnb_ref, wsc_ref, gc_ref, gs_ref = (next(it) for _ in range(7))
    out_refs = [next(it) for _ in outs]
    if conv:
        zcs_ref, h_ref, u_ref, v_ref = next(it), next(it), next(it), next(it)
    else:
        h_ref = next(it)

    gain = g_ref[layer, :] * (1.0 + sc_ref[0])
    shift = sh_ref[0]

    def modulate(xv):
        return (_rms(xv) * gain + shift).astype(BF16)

    if not conv:
        h_ref[...] = modulate(x_ref[0])
        for _ in _project(h_ref, slice(0, tm), w_refs, out_refs, outs, rope, cos, sin):
            pass
        return

    i = pl.program_id(1)
    top_ok = (i > 0).astype(F32)
    bot_ok = (i < pl.num_programs(1) - 1).astype(F32)
    mid = slice(HALO, HALO + tm)
    top = slice(0, HALO)
    bot = slice(HALO + tm, HALO + tm + HALO)
    h_ref[mid, :] = modulate(x_ref[0])
    h_ref[top, :] = modulate(xp_ref[0])
    h_ref[bot, :] = modulate(xn_ref[0])

    def rdot(k):
        return _dot(h_ref[...], w_ref[0, :, r_col + k * cw:r_col + (k + 1) * cw])

    def store_padded(ref, val):
        ref[mid, :] = val[mid]
        ref[top, :] = val[top] * top_ok
        ref[bot, :] = val[bot] * bot_ok

    cv, cg = rdot(0), rdot(1)
    store_padded(u_ref, cv * jax.nn.sigmoid(cg))

    rc = min(CONV_RC, tm)
    win = rc + 2 * HALO
    pad_c = conf_k // 2

    def conformer_chunk(c, after):
        r0 = c * rc
        zero = None
        if after is not None:
            half = jnp.uint32(16)
            bits = lax.shift_right_logical(lax.shift_right_logical(pltpu.bitcast(after, jnp.uint32), half), half)
            zero = pltpu.bitcast(bits, F32)[0:1, 0:1]
        halves = []
        for hh in range(cw // V7X_LANES):
            lanes = slice(hh * V7X_LANES, (hh + 1) * V7X_LANES)
            wdw = wcf_ref[0, :, lanes]
            window = u_ref[r0:r0 + win, lanes]
            acc = None
            for res in range(V7X_SUBLANES):
                shifted = window if res == 0 else pltpu.roll(window, win - res, 0)
                for k in range(conf_k):
                    off = HALO - pad_c + k
                    if off % V7X_SUBLANES != res:
                        continue
                    term = shifted[off - res:off - res + rc, :] * wdw[k:k + 1, :]
                    acc = term if acc is None else acc + term
            halves.append(acc)
        bias = bcf_ref[layer, :] if zero is None else bcf_ref[layer, :] + zero
        y = jnp.concatenate(halves, axis=-1) + bias
        cen = y - jnp.mean(y, axis=-1, keepdims=True)
        var = jnp.mean(cen * cen, axis=-1, keepdims=True)
        ln = cen * lax.rsqrt(var + EPS) * lng_ref[layer, :] + lnb_ref[layer, :]
        cf = ln * jax.nn.sigmoid(ln)
        zcs_ref[0, r0:r0 + rc, 0:cw] = (_rms(cf) * gc_ref[layer, :]).astype(BF16)

    def short_conv(sb):
        vwin = v_ref[...]
        wsc = wsc_ref[0]
        s = sb[mid] * (pltpu.roll(vwin, 1, 0)[mid] * wsc[0:1, :] + vwin[mid] * wsc[1:2, :]
                       + pltpu.roll(vwin, tm + 2 * HALO - 1, 0)[mid] * wsc[2:3, :])
        zcs_ref[0, :, cw:2 * cw] = (_rms(s) * gs_ref[layer, :]).astype(BF16)

    piece = lambda val: val[0:V7X_SUBLANES, 0:V7X_LANES]
    short_in = {}

    def mxu_work():
        short_in["sb"] = rdot(2)
        yield piece(short_in["sb"])
        scg = rdot(3)
        yield piece(scg)
        su = rdot(4)
        store_padded(v_ref, scg * su)
        yield piece(su)
        yield from _project(h_ref, mid, w_refs, out_refs, outs, rope, cos, sin)

    work = mxu_work()
    pending = None
    for c in range(tm // rc):
        after, pending = pending, next(work, None)
        conformer_chunk(c, after)
    for _ in work:
        pass
    short_conv(short_in["sb"])


def _mixer_in(x, mod, row, per_batch, g_pre, w, w_kv, l, tables, outs, conv_params, tm):
    b, t, d = x.shape
    n = w.shape[2]
    rope = tables is not None
    conv = conv_params is not None
    widths = [o[2] for o in outs]
    assert t % tm == 0 and tm % HALO == 0 and all(o[1] + o[2] <= (w, w_kv)[o[0]].shape[2] for o in outs)
    tile = lambda wd: pl.BlockSpec((1, tm, wd), lambda bb, i: (bb, i, 0))
    in_specs, args = [], []
    if conv:
        prev, nxt = _halo_specs(tm, t, d)
        in_specs += [prev, tile(d), nxt]
        args += [x, x, x]
    else:
        in_specs += [tile(d)]
        args += [x]
    in_specs += [_mod_spec(row, 0, d, per_batch), _mod_spec(row, 1, d, per_batch),
                 _rows_spec(g_pre), _layer_spec(w, l, single_buffer=True),
                 _layer_spec(w_kv, l, single_buffer=True)]
    args += [mod, mod, g_pre, w, w_kv]
    if rope:
        in_specs += [pl.BlockSpec((tm, V7X_LANES), lambda bb, i: (i, 0))] * 2
        args += list(tables)
    is_t = [o[3] in ("v_lo", "v_hi") for o in outs]
    out_shape = [jax.ShapeDtypeStruct((b, wd, t) if tr else (b, t, wd), BF16) for wd, tr in zip(widths, is_t)]
    out_specs = [pl.BlockSpec((1, wd, tm), lambda bb, i: (bb, 0, i)) if tr else tile(wd)
                 for wd, tr in zip(widths, is_t)]
    scratch = [pltpu.VMEM((tm, d), BF16)]
    conf_k = cw = r_col = 0
    if conv:
        w_conf_dw, b_conf_dw, ln_g, ln_b, w_sc_dw, g_group = conv_params
        conf_k, cw = w_conf_dw.shape[1:]
        r_col = n - 5 * cw
        assert w_sc_dw.shape[2] == cw and conf_k // 2 < HALO and g_group.shape[1] == ATTN_W + 2 * cw
        in_specs += [_layer_spec(w_conf_dw, l), _rows_spec(b_conf_dw), _rows_spec(ln_g), _rows_spec(ln_b),
                     _layer_spec(w_sc_dw, l),
                     _rows_spec(g_group, cols=cw, col_block=ATTN_W // cw),
                     _rows_spec(g_group, cols=cw, col_block=ATTN_W // cw + 1)]
        args += [w_conf_dw, b_conf_dw, ln_g, ln_b, w_sc_dw, g_group, g_group]
        out_shape.append(jax.ShapeDtypeStruct((b, t, 2 * cw), BF16))
        out_specs.append(tile(2 * cw))
        rows = tm + 2 * HALO
        scratch = [pltpu.VMEM((rows, d), BF16), pltpu.VMEM((rows, cw), F32), pltpu.VMEM((rows, cw), F32)]
    kern = functools.partial(_mixer_in_kernel, l=l, outs=tuple(outs), rope=rope, conv=conv, r_col=r_col,
                             tm=tm, conf_k=conf_k, cw=cw)
    return pl.pallas_call(
        kern,
        out_shape=out_shape,
        grid=(b, t // tm),
        in_specs=in_specs,
        out_specs=out_specs,
        scratch_shapes=scratch,
        compiler_params=_params("parallel", "parallel"),
        name="mixer_in",
    )(*args)


def _mix_kernel(*refs, band, t, tq, l):
    if band:
        (sink_ref, q_ref, kk_ref, vlo_ref, vhi_ref, kc_ref, vclo_ref, vchi_ref, zcs_ref, x_ref,
         gt_ref, sc2_ref, sh2_ref, ga_ref, gpost_ref, gffn_ref, wout_ref, xo_ref, h2_ref) = refs
    else:
        (sink_ref, q_ref, kc_ref, vclo_ref, vchi_ref, zcs_ref, x_ref,
         gt_ref, sc2_ref, sh2_ref, ga_ref, gpost_ref, gffn_ref, wout_ref, xo_ref, h2_ref) = refs
    blk = ATTN_BLOCK
    band_w = blk + 2 * WINDOW
    lane = lax.broadcasted_iota(jnp.int32, (1, V7X_LANES), 1)
    lo_lanes = lane < HEAD_DIM
    m_lo = jnp.where(lo_lanes, 1.0, 0.0).astype(BF16)
    m_hi = jnp.where(lo_lanes, 0.0, 1.0).astype(BF16)
    t0 = pl.program_id(1) * tq
    n_blocks = tq // blk
    groups = [(qb, h) for qb in range(n_blocks) for h in range(N_KV_HEADS)]

    def window(qb):
        q0 = t0 + qb * blk
        return q0, pl.multiple_of(jnp.clip(q0 - WINDOW, 0, t - band_w), blk)

    def scores(qb, h):
        rows = slice(qb * blk, (qb + 1) * blk)
        kv_lanes = slice(h * V7X_LANES, (h + 1) * V7X_LANES)
        qp0 = q_ref[0, rows, (2 * h) * V7X_LANES:(2 * h + 1) * V7X_LANES]
        qp1 = q_ref[0, rows, (2 * h + 1) * V7X_LANES:(2 * h + 2) * V7X_LANES]
        qs = jnp.concatenate([qp0 * m_lo, qp1 * m_lo, qp0 * m_hi, qp1 * m_hi], axis=0)
        s_ctx = _dot_nt(kc_ref[0, :, kv_lanes], qs)
        s_band = None
        if band:
            _, start = window(qb)
            s_band = _dot_nt(kk_ref[0, pl.ds(start, band_w), kv_lanes], qs)
        return s_ctx, s_band

    hidden = {}
    dims_first = lax.broadcasted_iota(jnp.int32, (V7X_LANES, 1), 0) < HEAD_DIM

    def attend(qb, h, s_ctx, s_band):
        kv_rows = slice(h * V7X_LANES, (h + 1) * V7X_LANES)
        heads = (4 * h, 4 * h + 2, 4 * h + 1, 4 * h + 3)
        if band:
            q0, start = window(qb)
            if qb not in hidden:
                rel = (lax.broadcasted_iota(jnp.int32, (band_w, blk), 0)
                       - lax.broadcasted_iota(jnp.int32, (band_w, blk), 1)) + (start - q0)
                hidden[qb] = jnp.where(jnp.abs(rel) <= WINDOW, 0.0, NEG)
        p_ctx, p_band, snk = [], [], []
        for r, head in enumerate(heads):
            cc = slice(r * blk, (r + 1) * blk)
            sink2 = sink_ref[l * N_Q_HEADS + head] * LOG2E
            sc = s_ctx[:, cc]
            m = jnp.maximum(jnp.max(sc, axis=0, keepdims=True), sink2)
            if band:
                sb = s_band[:, cc] + hidden[qb]
                m = jnp.maximum(m, jnp.max(sb, axis=0, keepdims=True))
                p_band.append(jnp.exp2(sb - m).astype(BF16))
            p_ctx.append(jnp.exp2(sc - m).astype(BF16))
            snk.append(jnp.exp2(sink2 - m))
        o_lo = _dot(vclo_ref[0, kv_rows, :], jnp.concatenate(p_ctx[0:2], axis=1))
        o_hi = _dot(vchi_ref[0, kv_rows, :], jnp.concatenate(p_ctx[2:4], axis=1))
        if band:
            o_lo = o_lo + _dot(vlo_ref[0, kv_rows, pl.ds(start, band_w)], jnp.concatenate(p_band[0:2], axis=1))
            o_hi = o_hi + _dot(vhi_ref[0, kv_rows, pl.ds(start, band_w)], jnp.concatenate(p_band[2:4], axis=1))
        num = jnp.where(dims_first, o_lo, o_hi)
        den = jnp.where(dims_first,
                        pltpu.roll(o_lo, HEAD_DIM, 0) + jnp.concatenate(snk[0:2], axis=1),
                        pltpu.roll(o_hi, HEAD_DIM, 0) + jnp.concatenate(snk[2:4], axis=1))
        o_t = num / den
        return [o_t[:, 0:blk].T, o_t[:, blk:2 * blk].T]

    merge_rows = min(tq, MIX_MERGE_ROWS)
    blocks_per_merge = merge_rows // blk
    layer = slice(l, l + 1)
    post_gain = gt_ref[0] * gpost_ref[layer, :]
    ffn_gain = gffn_ref[layer, :] * (1.0 + sc2_ref[0])

    def merge(mi, tiles):
        rows = slice(mi * merge_rows, (mi + 1) * merge_rows)
        a = jnp.concatenate(
            [jnp.concatenate([tl for h in range(N_KV_HEADS) for tl in tiles[(qb, h)]], axis=-1)
             for qb in range(mi * blocks_per_merge, (mi + 1) * blocks_per_merge)], axis=0)
        za = (_rms(a) * ga_ref[layer, :]).astype(BF16)
        y = _dot(za, wout_ref[0, 0:ATTN_W, :]) + _dot(zcs_ref[0, rows, :], wout_ref[0, ATTN_W:, :])
        x_mid = x_ref[0, rows, :] + _rms(y) * post_gain
        xo_ref[0, rows, :] = x_mid
        h2_ref[0, rows, :] = (_rms(x_mid) * ffn_gain + sh2_ref[0]).astype(BF16)

    tiles = {}
    nxt = scores(*groups[0])
    for gi, (qb, h) in enumerate(groups):
        cur = nxt
        if gi + 1 < len(groups):
            nxt = scores(*groups[gi + 1])
        tiles[(qb, h)] = attend(qb, h, *cur)
        if h == N_KV_HEADS - 1 and (qb + 1) % blocks_per_merge == 0:
            merge(qb // blocks_per_merge, tiles)


def _mix(sink, l, q, kk, vlo, vhi, kc, vclo, vchi, zcs, x, mod, row, per_batch,
         g_group, g_post, g_ffn, w_out, tq):
    b, t, d = x.shape
    band = kk is not None
    ctx_len = kc.shape[1]
    assert t % tq == 0 and tq % min(tq, MIX_MERGE_ROWS) == 0 and MIX_MERGE_ROWS % ATTN_BLOCK == 0
    tile = lambda w: pl.BlockSpec((1, tq, w), lambda bb, i: (bb, i, 0))
    whole = lambda n, w: pl.BlockSpec((1, n, w), lambda bb, i: (bb, 0, 0))
    in_specs = [pl.BlockSpec(memory_space=pltpu.SMEM), tile(ATTN_W)]
    args = [sink, q]
    whole_t = lambda n: pl.BlockSpec((1, KV_DUP_W, n), lambda bb, i: (bb, 0, 0))
    if band:
        in_specs += [whole(t, KV_DUP_W), whole_t(t), whole_t(t)]
        args += [kk, vlo, vhi]
    in_specs += [whole(ctx_len, KV_DUP_W), whole_t(ctx_len), whole_t(ctx_len)]
    in_specs += [tile(zcs.shape[2]), tile(d),
                 _mod_spec(row, 2, d, per_batch), _mod_spec(row, 4, d, per_batch),
                 _mod_spec(row, 3, d, per_batch),
                 _rows_spec(g_group, cols=ATTN_W), _rows_spec(g_post), _rows_spec(g_ffn),
                 _layer_spec(w_out, l)]
    args += [kc, vclo, vchi, zcs, x, mod, mod, mod, g_group, g_post, g_ffn, w_out]
    kern = functools.partial(_mix_kernel, band=band, t=t, tq=tq, l=l)
    return pl.pallas_call(
        kern,
        out_shape=[jax.ShapeDtypeStruct((b, t, d), F32), jax.ShapeDtypeStruct((b, t, d), BF16)],
        grid=(b, t // tq),
        in_specs=in_specs,
        out_specs=[tile(d), tile(d)],
        compiler_params=_params("parallel", "parallel"),
        name="mix",
    )(*args)


def _ffn_kernel(hp_ref, h_ref, hn_ref, x_ref, gt_ref, gpost_ref, wup_ref, wdw_ref, wdn_ref, o_ref,
                hcat_ref, act_ref, *, l, tm, d_ff, tf):
    i = pl.program_id(1)
    rows = tm + 2 * HALO
    hcat_ref[HALO:HALO + tm, :] = h_ref[0]

    @pl.when(i > 0)
    def _():
        hcat_ref[0:HALO, :] = hp_ref[0]

    @pl.when(i == 0)
    def _():
        hcat_ref[0:HALO, :] = jnp.zeros((HALO, hcat_ref.shape[1]), BF16)

    @pl.when(i < pl.num_programs(1) - 1)
    def _():
        hcat_ref[HALO + tm:, :] = hn_ref[0]

    @pl.when(i == pl.num_programs(1) - 1)
    def _():
        hcat_ref[HALO + tm:, :] = jnp.zeros((HALO, hcat_ref.shape[1]), BF16)

    hc = hcat_ref[...]
    mid = slice(HALO, HALO + tm)

    def conv3(u, w):
        return (pltpu.roll(u, 1, 0)[mid] * w[0:1, :] + u[mid] * w[1:2, :]
                + pltpu.roll(u, rows - 1, 0)[mid] * w[2:3, :])

    n_chunks = d_ff // tf

    def up(c):
        gcols = slice(c * tf, (c + 1) * tf)
        vcols = slice(d_ff + c * tf, d_ff + (c + 1) * tf)
        return _dot(hc, wup_ref[0, :, gcols]), _dot(hc, wup_ref[0, :, vcols])

    nxt = up(0)
    for c in range(n_chunks):
        ug, uv = nxt
        if c + 1 < n_chunks:
            nxt = up(c + 1)
        gcols = slice(c * tf, (c + 1) * tf)
        vcols = slice(d_ff + c * tf, d_ff + (c + 1) * tf)
        gate = conv3(ug, wdw_ref[0, :, gcols])
        val = conv3(uv, wdw_ref[0, :, vcols])
        act_ref[:, gcols] = (gate * jax.nn.sigmoid(gate) * val).astype(BF16)
    y = _dot(act_ref[...], wdn_ref[0])
    o_ref[0] = x_ref[0] + (gt_ref[0] * gpost_ref[l:l + 1, :]) * _rms(y)


def _ffn(h2, x, mod, row, per_batch, l, g_post, w_up, w_dw, w_down, tm):
    b, t, d = x.shape
    d_ff = w_down.shape[1]
    assert t % tm == 0 and tm % HALO == 0 and d_ff % FFN_TF == 0
    prev, nxt = _halo_specs(tm, t, d)
    tile = pl.BlockSpec((1, tm, d), lambda bb, i: (bb, i, 0))
    kern = functools.partial(_ffn_kernel, l=l, tm=tm, d_ff=d_ff, tf=FFN_TF)
    return pl.pallas_call(
        kern,
        out_shape=jax.ShapeDtypeStruct((b, t, d), F32),
        grid=(b, t // tm),
        in_specs=[prev, tile, nxt, tile, _mod_spec(row, 5, d, per_batch), _rows_spec(g_post),
                  _layer_spec(w_up, l, single_buffer=True), _layer_spec(w_dw, l),
                  _layer_spec(w_down, l, single_buffer=True)],
        out_specs=tile,
        scratch_shapes=[pltpu.VMEM((tm + 2 * HALO, d), BF16), pltpu.VMEM((tm, d_ff), BF16)],
        compiler_params=_params("parallel", "parallel"),
        name="conv_ffn",
    )(h2, h2, h2, x, mod, g_post, w_up, w_dw, w_down)


def _rope_tables(n_tokens):
    lane = np.arange(V7X_LANES)
    row_axis = ((lane % HEAD_DIM) // (2 * ROPE_FREQS)) == 0
    sign = np.where((lane % (2 * ROPE_FREQS)) < ROPE_FREQS, -1.0, 1.0).astype(np.float32)
    inv = ROPE_THETA ** (-jnp.arange(ROPE_FREQS, dtype=F32) / ROPE_FREQS)
    tok = jnp.arange(n_tokens)
    pos = jnp.where(jnp.asarray(row_axis)[None, :], (tok // GRID_W)[:, None], (tok % GRID_W)[:, None])
    ang = pos.astype(F32) * jnp.tile(inv, V7X_LANES // ROPE_FREQS)[None, :]
    return jnp.cos(ang), jnp.sin(ang) * sign[None, :]


def _kv_dup_weights(w_in):
    depth, d, _ = w_in.shape
    heads = w_in[:, :, ATTN_W:ATTN_W + 2 * KV_W].astype(BF16).reshape(depth, d, 2 * N_KV_HEADS, 1, HEAD_DIM)
    return jnp.broadcast_to(heads, (depth, d, 2 * N_KV_HEADS, 2, HEAD_DIM)).reshape(depth, d, 2 * KV_DUP_W)


def kernel(x, c, ctx, c_ctx, w_ada, b_ada, g_pre_mix, g_post_mix, g_pre_ffn, g_post_ffn, w_in, sink,
           w_conf_dw, b_conf_dw, conf_ln_g, conf_ln_b, w_sc_dw, g_group, w_out, w_up, w_ffn_dw, w_down):
    batch, seq, d = x.shape
    depth = w_in.shape[0]
    ctx_len = ctx.shape[1]
    assert batch < MOD_ROWS

    cc = jnp.concatenate([c, c_ctx[None, :], jnp.zeros((MOD_ROWS - batch - 1, d), F32)], axis=0)
    mod = _modulation(cc, w_ada, b_ada).reshape(depth * MOD_ROWS, 1, N_MOD * d)
    tables = _rope_tables(seq)
    sink_flat = sink.reshape(-1).astype(F32)
    w_in_b, w_out_b, w_up_b, w_down_b = (w.astype(BF16) for w in (w_in, w_out, w_up, w_down))
    w_kv = _kv_dup_weights(w_in)
    conv_params = (w_conf_dw, b_conf_dw, conf_ln_g, conf_ln_b, w_sc_dw, g_group)

    kv_outs = [(1, 0, KV_DUP_W, "k"), (1, KV_DUP_W, KV_DUP_W, "v_lo"), (1, KV_DUP_W, KV_DUP_W, "v_hi")]
    all_outs = [(0, 0, ATTN_W, "q")] + kv_outs

    for l in range(depth):
        update_ctx = l < depth - 1
        row = l * MOD_ROWS
        q, kk, vlo, vhi, zcs = _mixer_in(x, mod, row, True, g_pre_mix, w_in_b, w_kv, l, tables, all_outs,
                                         conv_params, IN_TM)
        if update_ctx:
            qc, kc, vclo, vchi, zcs_c = _mixer_in(ctx, mod, row + batch, False, g_pre_mix, w_in_b, w_kv, l,
                                                  None, all_outs, conv_params, ctx_len)
        else:
            kc, vclo, vchi = _mixer_in(ctx, mod, row + batch, False, g_pre_mix, w_in_b, w_kv, l, None,
                                       kv_outs, None, ctx_len)
        x_mid, h2 = _mix(sink_flat, l, q, kk, vlo, vhi, kc, vclo, vchi, zcs, x, mod, row, True,
                         g_group, g_post_mix, g_pre_ffn, w_out_b, MIX_TQ)
        x = _ffn(h2, x_mid, mod, row, True, l, g_post_ffn, w_up_b, w_ffn_dw, w_down_b, FFN_TM)
        if update_ctx:
            ctx_mid, hc2 = _mix(sink_flat, l, qc, None, None, None, kc, vclo, vchi, zcs_c, ctx, mod,
                                row + batch, False, g_group, g_post_mix, g_pre_ffn, w_out_b, ctx_len)
            ctx = _ffn(hc2, ctx_mid, mod, row + batch, False, l, g_post_ffn, w_up_b, w_ffn_dw, w_down_b,
                       ctx_len)
    return x
```

```python
import functools

import numpy as np
import jax
import jax.numpy as jnp
from jax import lax
from jax.experimental import pallas as pl
from jax.experimental.pallas import tpu as pltpu

F32 = jnp.float32
BF16 = jnp.bfloat16

V7X_LANES = 128
V7X_SUBLANES = 8
V7X_BF16_SUBLANES = 16
V7X_VMEM_LIMIT_BYTES = 56 * 1024 * 1024

HEAD_DIM = 64
N_Q_HEADS = 8
N_KV_HEADS = 2
GRID_W = 64
WINDOW = 128
ATTN_BLOCK = 128
ROPE_FREQS = HEAD_DIM // 4
ROPE_THETA = 10000.0
EPS = 1e-6
NEG = -1e30
LOG2E = 1.4426950408889634

ATTN_W = N_Q_HEADS * HEAD_DIM
KV_W = N_KV_HEADS * HEAD_DIM
KV_DUP_W = 2 * KV_W
N_MOD = 6
MOD_ROWS = 16

IN_TM = 512
MIX_TQ = 512
MIX_MERGE_ROWS = 256
FFN_TM = 512
FFN_TF = 256
HALO = V7X_BF16_SUBLANES
CONV_RC = 64
PROJ_CHUNK = 2 * V7X_LANES
MOD_TN = 1536


def _rms(x, eps=EPS):
    return x * lax.rsqrt(jnp.mean(x * x, axis=-1, keepdims=True) + eps)


def _dot(a, b):
    return jnp.dot(a, b, preferred_element_type=F32)


def _dot_nt(a, b):
    return lax.dot_general(a, b, (((1,), (1,)), ((), ())), preferred_element_type=F32)


def _params(*sem):
    return pltpu.CompilerParams(dimension_semantics=sem, vmem_limit_bytes=V7X_VMEM_LIMIT_BYTES)


def _layer_spec(arr, l, cols=None, col_block=0, single_buffer=False):
    _, rows, width = arr.shape
    width = cols or width
    mode = dict(pipeline_mode=pl.Buffered(1)) if single_buffer else {}
    return pl.BlockSpec((1, rows, width), lambda *_: (l, 0, col_block), **mode)


def _rows_spec(arr, cols=None, col_block=0):
    depth, width = arr.shape
    return pl.BlockSpec((depth, cols or width), lambda *_: (0, col_block))


def _mod_spec(row, k, d, per_batch):
    if per_batch:
        return pl.BlockSpec((1, 1, d), lambda b, i: (row + b, 0, k))
    return pl.BlockSpec((1, 1, d), lambda b, i: (row, 0, k))


def _halo_specs(tm, t, d):
    per_tile = tm // HALO
    last = t // HALO - 1
    return (pl.BlockSpec((1, HALO, d), lambda bb, i: (bb, jnp.maximum(i * per_tile - 1, 0), 0)),
            pl.BlockSpec((1, HALO, d), lambda bb, i: (bb, jnp.minimum((i + 1) * per_tile, last), 0)))


def _mod_kernel(c_ref, w_ref, b_ref, o_ref):
    c = c_ref[...]
    a = c * jax.nn.sigmoid(c)
    w = w_ref[0]
    a_hi = a.astype(BF16)
    a_lo = (a - a_hi.astype(F32)).astype(BF16)
    w_hi = w.astype(BF16)
    w_lo = (w - w_hi.astype(F32)).astype(BF16)
    hi_lo = _dot(jnp.concatenate([a_hi, a_lo], axis=0), w_hi)
    acc = hi_lo[0:MOD_ROWS] + hi_lo[MOD_ROWS:2 * MOD_ROWS] + _dot(a_hi, w_lo)
    o_ref[0] = acc + b_ref[0]


def _modulation(cc, w_ada, b_ada):
    depth, d, n = w_ada.shape
    return pl.pallas_call(
        _mod_kernel,
        out_shape=jax.ShapeDtypeStruct((depth, MOD_ROWS, n), F32),
        grid=(depth, n // MOD_TN),
        in_specs=[
            pl.BlockSpec((MOD_ROWS, d), lambda l, j: (0, 0)),
            pl.BlockSpec((1, d, MOD_TN), lambda l, j: (l, 0, j)),
            pl.BlockSpec((1, 1, MOD_TN), lambda l, j: (l, 0, j)),
        ],
        out_specs=pl.BlockSpec((1, MOD_ROWS, MOD_TN), lambda l, j: (l, 0, j)),
        compiler_params=_params("parallel", "parallel"),
        name="modulation",
    )(cc, w_ada, b_ada.reshape(depth, 1, n))


def _project(h_ref, rows, w_refs, out_refs, outs, rope, cos, sin):
    lane = lax.broadcasted_iota(jnp.int32, (1, V7X_LANES), 1)
    dims_first = lax.broadcasted_iota(jnp.int32, (V7X_LANES, 1), 0) < HEAD_DIM
    first_half = (lane % (2 * ROPE_FREQS)) < ROPE_FREQS
    chunks = {}
    for o_ref, (wi, col, width, kind) in zip(out_refs, outs):
        for c0 in range(0, width, PROJ_CHUNK):
            chunks.setdefault((wi, col + c0, col + c0 + PROJ_CHUNK), []).append((o_ref, c0, kind))
    prev = None
    for cols, users in list(chunks.items()) + [(None, None)]:
        cur = (_dot(h_ref[rows, :], w_refs[cols[0]][0, :, cols[1]:cols[2]]), users) if users else None
        if prev is not None:
            p, p_users = prev
            transposed = {}
            for o_ref, c0, kind in p_users:
                tiles = []
                for j in range(PROJ_CHUNK // V7X_LANES):
                    t = p[:, j * V7X_LANES:(j + 1) * V7X_LANES]
                    if kind in ("v_lo", "v_hi"):
                        if j not in transposed:
                            transposed[j] = t.T
                        keep = dims_first if kind == "v_lo" else jnp.logical_not(dims_first)
                        t_t = jnp.where(keep, transposed[j], 1.0)
                        o_ref[0, c0 + j * V7X_LANES:c0 + (j + 1) * V7X_LANES, :] = t_t.astype(BF16)
                        continue
                    if rope:
                        partner = jnp.where(first_half,
                                            pltpu.roll(t, V7X_LANES - ROPE_FREQS, 1),
                                            pltpu.roll(t, ROPE_FREQS, 1))
                        t = t * cos + partner * sin
                    if kind == "q":
                        t = t * (HEAD_DIM ** -0.5 * LOG2E)
                    tiles.append(t.astype(BF16))
                if tiles:
                    o_ref[0, :, c0:c0 + PROJ_CHUNK] = jnp.concatenate(tiles, axis=-1)
        prev = cur
        yield None if cur is None else cur[0][0:V7X_SUBLANES, 0:V7X_LANES]


def _mixer_in_kernel(*refs, l, outs, rope, conv, r_col, tm, conf_k, cw):
    it = iter(refs)
    if conv:
        xp_ref, x_ref, xn_ref = next(it), next(it), next(it)
    else:
        x_ref = next(it)
    sh_ref, sc_ref, g_ref, w_ref, wkv_ref = next(it), next(it), next(it), next(it), next(it)
    w_refs = (w_ref, wkv_ref)
    layer = slice(l, l + 1)
    cos = sin = None
    if rope:
        cos, sin = next(it)[...], next(it)[...]
    if conv:
        wcf_ref, bcf_ref, lng_ref, lnb_ref, wsc_ref, gc_ref, gs_ref = (next(it) for _ in range(7))
    out_refs = [next(it) for _ in outs]
    if conv:
        zcs_ref, h_ref, u_ref, v_ref = next(it), next(it), next(it), next(it)
    else:
        h_ref = next(it)

    gain = g_ref[layer, :] * (1.0 + sc_ref[0])
    shift = sh_ref[0]

    def modulate(xv):
        return (_rms(xv) * gain + shift).astype(BF16)

    if not conv:
        h_ref[...] = modulate(x_ref[0])
        for _ in _project(h_ref, slice(0, tm), w_refs, out_refs, outs, rope, cos, sin):
            pass
        return

    i = pl.program_id(1)
    top_ok = (i > 0).astype(F32)
    bot_ok = (i < pl.num_programs(1) - 1).astype(F32)
    mid = slice(HALO, HALO + tm)
    top = slice(0, HALO)
    bot = slice(HALO + tm, HALO + tm + HALO)
    h_ref[mid, :] = modulate(x_ref[0])
    h_ref[top, :] = modulate(xp_ref[0])
    h_ref[bot, :] = modulate(xn_ref[0])

    def rdot(k):
        return _dot(h_ref[...], w_ref[0, :, r_col + k * cw:r_col + (k + 1) * cw])

    def store_padded(ref, val):
        ref[mid, :] = val[mid]
        ref[top, :] = val[top] * top_ok
        ref[bot, :] = val[bot] * bot_ok

    cv, cg = rdot(0), rdot(1)
    store_padded(u_ref, cv * jax.nn.sigmoid(cg))

    rc = min(CONV_RC, tm)
    win = rc + 2 * HALO
    pad_c = conf_k // 2

    def conformer_chunk(c, after):
        r0 = c * rc
        zero = None
        if after is not None:
            half = jnp.uint32(16)
            bits = lax.shift_right_logical(lax.shift_right_logical(pltpu.bitcast(after, jnp.uint32), half), half)
            zero = pltpu.bitcast(bits, F32)[0:1, 0:1]
        halves = []
        for hh in range(cw // V7X_LANES):
            lanes = slice(hh * V7X_LANES, (hh + 1) * V7X_LANES)
            wdw = wcf_ref[0, :, lanes]
            window = u_ref[r0:r0 + win, lanes]
            acc = None
            for res in range(V7X_SUBLANES):
                shifted = window if res == 0 else pltpu.roll(window, win - res, 0)
                for k in range(conf_k):
                    off = HALO - pad_c + k
                    if off % V7X_SUBLANES != res:
                        continue
                    term = shifted[off - res:off - res + rc, :] * wdw[k:k + 1, :]
                    acc = term if acc is None else acc + term
            halves.append(acc)
        bias = bcf_ref[layer, :] if zero is None else bcf_ref[layer, :] + zero
        y = jnp.concatenate(halves, axis=-1) + bias
        cen = y - jnp.mean(y, axis=-1, keepdims=True)
        var = jnp.mean(cen * cen, axis=-1, keepdims=True)
        ln = cen * lax.rsqrt(var + EPS) * lng_ref[layer, :] + lnb_ref[layer, :]
        cf = ln * jax.nn.sigmoid(ln)
        zcs_ref[0, r0:r0 + rc, 0:cw] = (_rms(cf) * gc_ref[layer, :]).astype(BF16)

    def short_conv(sb):
        vwin = v_ref[...]
        wsc = wsc_ref[0]
        s = sb[mid] * (pltpu.roll(vwin, 1, 0)[mid] * wsc[0:1, :] + vwin[mid] * wsc[1:2, :]
                       + pltpu.roll(vwin, tm + 2 * HALO - 1, 0)[mid] * wsc[2:3, :])
        zcs_ref[0, :, cw:2 * cw] = (_rms(s) * gs_ref[layer, :]).astype(BF16)

    piece = lambda val: val[0:V7X_SUBLANES, 0:V7X_LANES]
    short_in = {}

    def mxu_work():
        short_in["sb"] = rdot(2)
        yield piece(short_in["sb"])
        scg = rdot(3)
        yield piece(scg)
        su = rdot(4)
        store_padded(v_ref, scg * su)
        yield piece(su)
        yield from _project(h_ref, mid, w_refs, out_refs, outs, rope, cos, sin)

    work = mxu_work()
    pending = None
    for c in range(tm // rc):
        after, pending = pending, next(work, None)
        conformer_chunk(c, after)
    for _ in work:
        pass
    short_conv(short_in["sb"])


def _mixer_in(x, mod, row, per_batch, g_pre, w, w_kv, l, tables, outs, conv_params, tm):
    b, t, d = x.shape
    n = w.shape[2]
    rope = tables is not None
    conv = conv_params is not None
    widths = [o[2] for o in outs]
    assert t % tm == 0 and tm % HALO == 0 and all(o[1] + o[2] <= (w, w_kv)[o[0]].shape[2] for o in outs)
    tile = lambda wd: pl.BlockSpec((1, tm, wd), lambda bb, i: (bb, i, 0))
    in_specs, args = [], []
    if conv:
        prev, nxt = _halo_specs(tm, t, d)
        in_specs += [prev, tile(d), nxt]
        args += [x, x, x]
    else:
        in_specs += [tile(d)]
        args += [x]
    in_specs += [_mod_spec(row, 0, d, per_batch), _mod_spec(row, 1, d, per_batch),
                 _rows_spec(g_pre), _layer_spec(w, l, single_buffer=True),
                 _layer_spec(w_kv, l, single_buffer=True)]
    args += [mod, mod, g_pre, w, w_kv]
    if rope:
        in_specs += [pl.BlockSpec((tm, V7X_LANES), lambda bb, i: (i, 0))] * 2
        args += list(tables)
    is_t = [o[3] in ("v_lo", "v_hi") for o in outs]
    out_shape = [jax.ShapeDtypeStruct((b, wd, t) if tr else (b, t, wd), BF16) for wd, tr in zip(widths, is_t)]
    out_specs = [pl.BlockSpec((1, wd, tm), lambda bb, i: (bb, 0, i)) if tr else tile(wd)
                 for wd, tr in zip(widths, is_t)]
    scratch = [pltpu.VMEM((tm, d), BF16)]
    conf_k = cw = r_col = 0
    if conv:
        w_conf_dw, b_conf_dw, ln_g, ln_b, w_sc_dw, g_group = conv_params
        conf_k, cw = w_conf_dw.shape[1:]
        r_col = n - 5 * cw
        assert w_sc_dw.shape[2] == cw and conf_k // 2 < HALO and g_group.shape[1] == ATTN_W + 2 * cw
        in_specs += [_layer_spec(w_conf_dw, l), _rows_spec(b_conf_dw), _rows_spec(ln_g), _rows_spec(ln_b),
                     _layer_spec(w_sc_dw, l),
                     _rows_spec(g_group, cols=cw, col_block=ATTN_W // cw),
                     _rows_spec(g_group, cols=cw, col_block=ATTN_W // cw + 1)]
        args += [w_conf_dw, b_conf_dw, ln_g, ln_b, w_sc_dw, g_group, g_group]
        out_shape.append(jax.ShapeDtypeStruct((b, t, 2 * cw), BF16))
        out_specs.append(tile(2 * cw))
        rows = tm + 2 * HALO
        scratch = [pltpu.VMEM((rows, d), BF16), pltpu.VMEM((rows, cw), F32), pltpu.VMEM((rows, cw), F32)]
    kern = functools.partial(_mixer_in_kernel, l=l, outs=tuple(outs), rope=rope, conv=conv, r_col=r_col,
                             tm=tm, conf_k=conf_k, cw=cw)
    return pl.pallas_call(
        kern,
        out_shape=out_shape,
        grid=(b, t // tm),
        in_specs=in_specs,
        out_specs=out_specs,
        scratch_shapes=scratch,
        compiler_params=_params("parallel", "parallel"),
        name="mixer_in",
    )(*args)


def _mix_kernel(*refs, band, t, tq, l):
    if band:
        (sink_ref, q_ref, kk_ref, vlo_ref, vhi_ref, kc_ref, vclo_ref, vchi_ref, zcs_ref, x_ref,
         gt_ref, sc2_ref, sh2_ref, ga_ref, gpost_ref, gffn_ref, wout_ref, xo_ref, h2_ref) = refs
    else:
        (sink_ref, q_ref, kc_ref, vclo_ref, vchi_ref, zcs_ref, x_ref,
         gt_ref, sc2_ref, sh2_ref, ga_ref, gpost_ref, gffn_ref, wout_ref, xo_ref, h2_ref) = refs
    blk = ATTN_BLOCK
    band_w = blk + 2 * WINDOW
    lane = lax.broadcasted_iota(jnp.int32, (1, V7X_LANES), 1)
    lo_lanes = lane < HEAD_DIM
    m_lo = jnp.where(lo_lanes, 1.0, 0.0).astype(BF16)
    m_hi = jnp.where(lo_lanes, 0.0, 1.0).astype(BF16)
    t0 = pl.program_id(1) * tq
    n_blocks = tq // blk
    groups = [(qb, h) for qb in range(n_blocks) for h in range(N_KV_HEADS)]

    def window(qb):
        q0 = t0 + qb * blk
        return q0, pl.multiple_of(jnp.clip(q0 - WINDOW, 0, t - band_w), blk)

    def scores(qb, h):
        rows = slice(qb * blk, (qb + 1) * blk)
        kv_lanes = slice(h * V7X_LANES, (h + 1) * V7X_LANES)
        qp0 = q_ref[0, rows, (2 * h) * V7X_LANES:(2 * h + 1) * V7X_LANES]
        qp1 = q_ref[0, rows, (2 * h + 1) * V7X_LANES:(2 * h + 2) * V7X_LANES]
        qs = jnp.concatenate([qp0 * m_lo, qp1 * m_lo, qp0 * m_hi, qp1 * m_hi], axis=0)
        s_ctx = _dot_nt(kc_ref[0, :, kv_lanes], qs)
        s_band = None
        if band:
            _, start = window(qb)
            s_band = _dot_nt(kk_ref[0, pl.ds(start, band_w), kv_lanes], qs)
        return s_ctx, s_band

    hidden = {}
    dims_first = lax.broadcasted_iota(jnp.int32, (V7X_LANES, 1), 0) < HEAD_DIM

    def attend(qb, h, s_ctx, s_band):
        kv_rows = slice(h * V7X_LANES, (h + 1) * V7X_LANES)
        heads = (4 * h, 4 * h + 2, 4 * h + 1, 4 * h + 3)
        if band:
            q0, start = window(qb)
            if qb not in hidden:
                rel = (lax.broadcasted_iota(jnp.int32, (band_w, blk), 0)
                       - lax.broadcasted_iota(jnp.int32, (band_w, blk), 1)) + (start - q0)
                hidden[qb] = jnp.where(jnp.abs(rel) <= WINDOW, 0.0, NEG)
        p_ctx, p_band, snk = [], [], []
        for r, head in enumerate(heads):
            cc = slice(r * blk, (r + 1) * blk)
            sink2 = sink_ref[l * N_Q_HEADS + head] * LOG2E
            sc = s_ctx[:, cc]
            m = jnp.maximum(jnp.max(sc, axis=0, keepdims=True), sink2)
            if band:
                sb = s_band[:, cc] + hidden[qb]
                m = jnp.maximum(m, jnp.max(sb, axis=0, keepdims=True))
                p_band.append(jnp.exp2(sb - m).astype(BF16))
            p_ctx.append(jnp.exp2(sc - m).astype(BF16))
            snk.append(jnp.exp2(sink2 - m))
        o_lo = _dot(vclo_ref[0, kv_rows, :], jnp.concatenate(p_ctx[0:2], axis=1))
        o_hi = _dot(vchi_ref[0, kv_rows, :], jnp.concatenate(p_ctx[2:4], axis=1))
        if band:
            o_lo = o_lo + _dot(vlo_ref[0, kv_rows, pl.ds(start, band_w)], jnp.concatenate(p_band[0:2], axis=1))
            o_hi = o_hi + _dot(vhi_ref[0, kv_rows, pl.ds(start, band_w)], jnp.concatenate(p_band[2:4], axis=1))
        num = jnp.where(dims_first, o_lo, o_hi)
        den = jnp.where(dims_first,
                        pltpu.roll(o_lo, HEAD_DIM, 0) + jnp.concatenate(snk[0:2], axis=1),
                        pltpu.roll(o_hi, HEAD_DIM, 0) + jnp.concatenate(snk[2:4], axis=1))
        o_t = num / den
        return [o_t[:, 0:blk].T, o_t[:, blk:2 * blk].T]

    merge_rows = min(tq, MIX_MERGE_ROWS)
    blocks_per_merge = merge_rows // blk
    layer = slice(l, l + 1)
    post_gain = gt_ref[0] * gpost_ref[layer, :]
    ffn_gain = gffn_ref[layer, :] * (1.0 + sc2_ref[0])

    def merge(mi, tiles):
        rows = slice(mi * merge_rows, (mi + 1) * merge_rows)
        a = jnp.concatenate(
            [jnp.concatenate([tl for h in range(N_KV_HEADS) for tl in tiles[(qb, h)]], axis=-1)
             for qb in range(mi * blocks_per_merge, (mi + 1) * blocks_per_merge)], axis=0)
        za = (_rms(a) * ga_ref[layer, :]).astype(BF16)
        y = _dot(za, wout_ref[0, 0:ATTN_W, :]) + _dot(zcs_ref[0, rows, :], wout_ref[0, ATTN_W:, :])
        x_mid = x_ref[0, rows, :] + _rms(y) * post_gain
        xo_ref[0, rows, :] = x_mid
        h2_ref[0, rows, :] = (_rms(x_mid) * ffn_gain + sh2_ref[0]).astype(BF16)

    tiles = {}
    nxt = scores(*groups[0])
    for gi, (qb, h) in enumerate(groups):
        cur = nxt
        if gi + 1 < len(groups):
            nxt = scores(*groups[gi + 1])
        tiles[(qb, h)] = attend(qb, h, *cur)
        if h == N_KV_HEADS - 1 and (qb + 1) % blocks_per_merge == 0:
            merge(qb // blocks_per_merge, tiles)


def _mix(sink, l, q, kk, vlo, vhi, kc, vclo, vchi, zcs, x, mod, row, per_batch,
         g_group, g_post, g_ffn, w_out, tq):
    b, t, d = x.shape
    band = kk is not None
    ctx_len = kc.shape[1]
    assert t % tq == 0 and tq % min(tq, MIX_MERGE_ROWS) == 0 and MIX_MERGE_ROWS % ATTN_BLOCK == 0
    tile = lambda w: pl.BlockSpec((1, tq, w), lambda bb, i: (bb, i, 0))
    whole = lambda n, w: pl.BlockSpec((1, n, w), lambda bb, i: (bb, 0, 0))
    in_specs = [pl.BlockSpec(memory_space=pltpu.SMEM), tile(ATTN_W)]
    args = [sink, q]
    whole_t = lambda n: pl.BlockSpec((1, KV_DUP_W, n), lambda bb, i: (bb, 0, 0))
    if band:
        in_specs += [whole(t, KV_DUP_W), whole_t(t), whole_t(t)]
        args += [kk, vlo, vhi]
    in_specs += [whole(ctx_len, KV_DUP_W), whole_t(ctx_len), whole_t(ctx_len)]
    in_specs += [tile(zcs.shape[2]), tile(d),
                 _mod_spec(row, 2, d, per_batch), _mod_spec(row, 4, d, per_batch),
                 _mod_spec(row, 3, d, per_batch),
                 _rows_spec(g_group, cols=ATTN_W), _rows_spec(g_post), _rows_spec(g_ffn),
                 _layer_spec(w_out, l)]
    args += [kc, vclo, vchi, zcs, x, mod, mod, mod, g_group, g_post, g_ffn, w_out]
    kern = functools.partial(_mix_kernel, band=band, t=t, tq=tq, l=l)
    return pl.pallas_call(
        kern,
        out_shape=[jax.ShapeDtypeStruct((b, t, d), F32), jax.ShapeDtypeStruct((b, t, d), BF16)],
        grid=(b, t // tq),
        in_specs=in_specs,
        out_specs=[tile(d), tile(d)],
        compiler_params=_params("parallel", "parallel"),
        name="mix",
    )(*args)


def _ffn_kernel(hp_ref, h_ref, hn_ref, x_ref, gt_ref, gpost_ref, wup_ref, wdw_ref, wdn_ref, o_ref,
                hcat_ref, act_ref, *, l, tm, d_ff, tf):
    i = pl.program_id(1)
    rows = tm + 2 * HALO
    hcat_ref[HALO:HALO + tm, :] = h_ref[0]

    @pl.when(i > 0)
    def _():
        hcat_ref[0:HALO, :] = hp_ref[0]

    @pl.when(i == 0)
    def _():
        hcat_ref[0:HALO, :] = jnp.zeros((HALO, hcat_ref.shape[1]), BF16)

    @pl.when(i < pl.num_programs(1) - 1)
    def _():
        hcat_ref[HALO + tm:, :] = hn_ref[0]

    @pl.when(i == pl.num_programs(1) - 1)
    def _():
        hcat_ref[HALO + tm:, :] = jnp.zeros((HALO, hcat_ref.shape[1]), BF16)

    hc = hcat_ref[...]
    mid = slice(HALO, HALO + tm)

    def conv3(u, w):
        return (pltpu.roll(u, 1, 0)[mid] * w[0:1, :] + u[mid] * w[1:2, :]
                + pltpu.roll(u, rows - 1, 0)[mid] * w[2:3, :])

    n_chunks = d_ff // tf

    def up(c):
        gcols = slice(c * tf, (c + 1) * tf)
        vcols = slice(d_ff + c * tf, d_ff + (c + 1) * tf)
        return _dot(hc, wup_ref[0, :, gcols]), _dot(hc, wup_ref[0, :, vcols])

    nxt = up(0)
    for c in range(n_chunks):
        ug, uv = nxt
        if c + 1 < n_chunks:
            nxt = up(c + 1)
        gcols = slice(c * tf, (c + 1) * tf)
        vcols = slice(d_ff + c * tf, d_ff + (c + 1) * tf)
        gate = conv3(ug, wdw_ref[0, :, gcols])
        val = conv3(uv, wdw_ref[0, :, vcols])
        act_ref[:, gcols] = (gate * jax.nn.sigmoid(gate) * val).astype(BF16)
    y = _dot(act_ref[...], wdn_ref[0])
    o_ref[0] = x_ref[0] + (gt_ref[0] * gpost_ref[l:l + 1, :]) * _rms(y)


def _ffn(h2, x, mod, row, per_batch, l, g_post, w_up, w_dw, w_down, tm):
    b, t, d = x.shape
    d_ff = w_down.shape[1]
    assert t % tm == 0 and tm % HALO == 0 and d_ff % FFN_TF == 0
    prev, nxt = _halo_specs(tm, t, d)
    tile = pl.BlockSpec((1, tm, d), lambda bb, i: (bb, i, 0))
    kern = functools.partial(_ffn_kernel, l=l, tm=tm, d_ff=d_ff, tf=FFN_TF)
    return pl.pallas_call(
        kern,
        out_shape=jax.ShapeDtypeStruct((b, t, d), F32),
        grid=(b, t // tm),
        in_specs=[prev, tile, nxt, tile, _mod_spec(row, 5, d, per_batch), _rows_spec(g_post),
                  _layer_spec(w_up, l, single_buffer=True), _layer_spec(w_dw, l),
                  _layer_spec(w_down, l, single_buffer=True)],
        out_specs=tile,
        scratch_shapes=[pltpu.VMEM((tm + 2 * HALO, d), BF16), pltpu.VMEM((tm, d_ff), BF16)],
        compiler_params=_params("parallel", "parallel"),
        name="conv_ffn",
    )(h2, h2, h2, x, mod, g_post, w_up, w_dw, w_down)


def _rope_tables(n_tokens):
    lane = np.arange(V7X_LANES)
    row_axis = ((lane % HEAD_DIM) // (2 * ROPE_FREQS)) == 0
    sign = np.where((lane % (2 * ROPE_FREQS)) < ROPE_FREQS, -1.0, 1.0).astype(np.float32)
    inv = ROPE_THETA ** (-jnp.arange(ROPE_FREQS, dtype=F32) / ROPE_FREQS)
    tok = jnp.arange(n_tokens)
    pos = jnp.where(jnp.asarray(row_axis)[None, :], (tok // GRID_W)[:, None], (tok % GRID_W)[:, None])
    ang = pos.astype(F32) * jnp.tile(inv, V7X_LANES // ROPE_FREQS)[None, :]
    return jnp.cos(ang), jnp.sin(ang) * sign[None, :]


def _kv_dup_weights(w_in):
    depth, d, _ = w_in.shape
    heads = w_in[:, :, ATTN_W:ATTN_W + 2 * KV_W].astype(BF16).reshape(depth, d, 2 * N_KV_HEADS, 1, HEAD_DIM)
    return jnp.broadcast_to(heads, (depth, d, 2 * N_KV_HEADS, 2, HEAD_DIM)).reshape(depth, d, 2 * KV_DUP_W)


def kernel(x, c, ctx, c_ctx, w_ada, b_ada, g_pre_mix, g_post_mix, g_pre_ffn, g_post_ffn, w_in, sink,
           w_conf_dw, b_conf_dw, conf_ln_g, conf_ln_b, w_sc_dw, g_group, w_out, w_up, w_ffn_dw, w_down):
    batch, seq, d = x.shape
    depth = w_in.shape[0]
    ctx_len = ctx.shape[1]
    assert batch < MOD_ROWS

    cc = jnp.concatenate([c, c_ctx[None, :], jnp.zeros((MOD_ROWS - batch - 1, d), F32)], axis=0)
    mod = _modulation(cc, w_ada, b_ada).reshape(depth * MOD_ROWS, 1, N_MOD * d)
    tables = _rope_tables(seq)
    sink_flat = sink.reshape(-1).astype(F32)
    w_in_b, w_out_b, w_up_b, w_down_b = (w.astype(BF16) for w in (w_in, w_out, w_up, w_down))
    w_kv = _kv_dup_weights(w_in)
    conv_params = (w_conf_dw, b_conf_dw, conf_ln_g, conf_ln_b, w_sc_dw, g_group)

    kv_outs = [(1, 0, KV_DUP_W, "k"), (1, KV_DUP_W, KV_DUP_W, "v_lo"), (1, KV_DUP_W, KV_DUP_W, "v_hi")]
    all_outs = [(0, 0, ATTN_W, "q")] + kv_outs

    for l in range(depth):
        update_ctx = l < depth - 1
        row = l * MOD_ROWS
        q, kk, vlo, vhi, zcs = _mixer_in(x, mod, row, True, g_pre_mix, w_in_b, w_kv, l, tables, all_outs,
                                         conv_params, IN_TM)
        if update_ctx:
            qc, kc, vclo, vchi, zcs_c = _mixer_in(ctx, mod, row + batch, False, g_pre_mix, w_in_b, w_kv, l,
                                                  None, all_outs, conv_params, ctx_len)
        else:
            kc, vclo, vchi = _mixer_in(ctx, mod, row + batch, False, g_pre_mix, w_in_b, w_kv, l, None,
                                       kv_outs, None, ctx_len)
        x_mid, h2 = _mix(sink_flat, l, q, kk, vlo, vhi, kc, vclo, vchi, zcs, x, mod, row, True,
                         g_group, g_post_mix, g_pre_ffn, w_out_b, MIX_TQ)
        x = _ffn(h2, x_mid, mod, row, True, l, g_post_ffn, w_up_b, w_ffn_dw, w_down_b, FFN_TM)
        if update_ctx:
            ctx_mid, hc2 = _mix(sink_flat, l, qc, None, None, None, kc, vclo, vchi, zcs_c, ctx, mod,
                                row + batch, False, g_group, g_post_mix, g_pre_ffn, w_out_b, ctx_len)
            ctx = _ffn(hc2, ctx_mid, mod, row + batch, False, l, g_post_ffn, w_up_b, w_ffn_dw, w_down_b,
                       ctx_len)
    return x
```

```python
import functools

import numpy as np
import jax
import jax.numpy as jnp
from jax import lax
from jax.experimental import pallas as pl
from jax.experimental.pallas import tpu as pltpu

F32 = jnp.float32
BF16 = jnp.bfloat16

V7X_LANES = 128
V7X_SUBLANES = 8
V7X_BF16_SUBLANES = 16
V7X_VMEM_LIMIT_BYTES = 56 * 1024 * 1024

HEAD_DIM = 64
N_Q_HEADS = 8
N_KV_HEADS = 2
GRID_W = 64
WINDOW = 128
ATTN_BLOCK = 128
ROPE_FREQS = HEAD_DIM // 4
ROPE_THETA = 10000.0
EPS = 1e-6
NEG = -1e30
LOG2E = 1.4426950408889634

ATTN_W = N_Q_HEADS * HEAD_DIM
KV_W = N_KV_HEADS * HEAD_DIM
KV_DUP_W = 2 * KV_W
N_MOD = 6
MOD_ROWS = 16

IN_TM = 512
MIX_TQ = 512
MIX_MERGE_ROWS = 256
FFN_TM = 512
FFN_TF = 256
HALO = V7X_BF16_SUBLANES
CONV_RC = 64
PROJ_CHUNK = 2 * V7X_LANES
MOD_TN = 1536


def _rms(x, eps=EPS):
    return x * lax.rsqrt(jnp.mean(x * x, axis=-1, keepdims=True) + eps)


def _dot(a, b):
    return jnp.dot(a, b, preferred_element_type=F32)


def _dot_nt(a, b):
    return lax.dot_general(a, b, (((1,), (1,)), ((), ())), preferred_element_type=F32)


def _params(*sem):
    return pltpu.CompilerParams(dimension_semantics=sem, vmem_limit_bytes=V7X_VMEM_LIMIT_BYTES)


def _layer_spec(arr, l, cols=None, col_block=0, single_buffer=False):
    _, rows, width = arr.shape
    width = cols or width
    mode = dict(pipeline_mode=pl.Buffered(1)) if single_buffer else {}
    return pl.BlockSpec((1, rows, width), lambda *_: (l, 0, col_block), **mode)


def _rows_spec(arr, cols=None, col_block=0):
    depth, width = arr.shape
    return pl.BlockSpec((depth, cols or width), lambda *_: (0, col_block))


def _mod_spec(row, k, d, per_batch):
    if per_batch:
        return pl.BlockSpec((1, 1, d), lambda b, i: (row + b, 0, k))
    return pl.BlockSpec((1, 1, d), lambda b, i: (row, 0, k))


def _cast_slots(weights, grid):
    steps = grid[0] * grid[1]
    in_specs, args, out_shapes, out_specs = [], [], [], []
    for arr, layer in weights:
        _, r, c = arr.shape
        span = 1
        while r % (steps // span) or (r // (steps // span)) % V7X_BF16_SUBLANES:
            span *= 2
            assert span <= steps, (r, steps)
        rows = r // (steps // span)
        in_specs.append(pl.BlockSpec((1, rows, c), lambda bb, i, layer=layer, span=span:
                                     (layer, (bb * grid[1] + i) // span, 0)))
        out_specs.append(pl.BlockSpec((1, rows, c), lambda bb, i, span=span: (0, (bb * grid[1] + i) // span, 0)))
        out_shapes.append(jax.ShapeDtypeStruct((1, r, c), BF16))
        args.append(arr)
    return in_specs, args, out_shapes, out_specs


def _cast_rows(cast_in, cast_out):
    for src, dst in zip(cast_in, cast_out):
        dst[...] = src[...].astype(BF16)


def _halo_specs(tm, t, d):
    per_tile = tm // HALO
    last = t // HALO - 1
    return (pl.BlockSpec((1, HALO, d), lambda bb, i: (bb, jnp.maximum(i * per_tile - 1, 0), 0)),
            pl.BlockSpec((1, HALO, d), lambda bb, i: (bb, jnp.minimum((i + 1) * per_tile, last), 0)))


def _mod_kernel(c_ref, w_ref, b_ref, o_ref):
    c = c_ref[...]
    a = c * jax.nn.sigmoid(c)
    w = w_ref[0]
    a_hi = a.astype(BF16)
    a_lo = (a - a_hi.astype(F32)).astype(BF16)
    w_hi = w.astype(BF16)
    w_lo = (w - w_hi.astype(F32)).astype(BF16)
    hi_lo = _dot(jnp.concatenate([a_hi, a_lo], axis=0), w_hi)
    acc = hi_lo[0:MOD_ROWS] + hi_lo[MOD_ROWS:2 * MOD_ROWS] + _dot(a_hi, w_lo)
    o_ref[0] = acc + b_ref[0]


def _modulation(cc, w_ada, b_ada):
    depth, d, n = w_ada.shape
    return pl.pallas_call(
        _mod_kernel,
        out_shape=jax.ShapeDtypeStruct((depth, MOD_ROWS, n), F32),
        grid=(depth, n // MOD_TN),
        in_specs=[
            pl.BlockSpec((MOD_ROWS, d), lambda l, j: (0, 0)),
            pl.BlockSpec((1, d, MOD_TN), lambda l, j: (l, 0, j)),
            pl.BlockSpec((1, 1, MOD_TN), lambda l, j: (l, 0, j)),
        ],
        out_specs=pl.BlockSpec((1, MOD_ROWS, MOD_TN), lambda l, j: (l, 0, j)),
        compiler_params=_params("parallel", "parallel"),
        name="modulation",
    )(cc, w_ada, b_ada.reshape(depth, 1, n))


def _project(h_ref, rows, w_refs, out_refs, outs, rope, cos, sin):
    lane = lax.broadcasted_iota(jnp.int32, (1, V7X_LANES), 1)
    dims_first = lax.broadcasted_iota(jnp.int32, (V7X_LANES, 1), 0) < HEAD_DIM
    first_half = (lane % (2 * ROPE_FREQS)) < ROPE_FREQS
    chunks = {}
    for o_ref, (wi, col, width, kind) in zip(out_refs, outs):
        for c0 in range(0, width, PROJ_CHUNK):
            chunks.setdefault((wi, col + c0, col + c0 + PROJ_CHUNK), []).append((o_ref, c0, kind))
    prev = None
    for cols, users in list(chunks.items()) + [(None, None)]:
        cur = (_dot(h_ref[rows, :], w_refs[cols[0]][0, :, cols[1]:cols[2]]), users) if users else None
        if prev is not None:
            p, p_users = prev
            transposed = {}
            for o_ref, c0, kind in p_users:
                tiles = []
                for j in range(PROJ_CHUNK // V7X_LANES):
                    t = p[:, j * V7X_LANES:(j + 1) * V7X_LANES]
                    if kind in ("v_lo", "v_hi"):
                        if j not in transposed:
                            transposed[j] = t.T
                        keep = dims_first if kind == "v_lo" else jnp.logical_not(dims_first)
                        t_t = jnp.where(keep, transposed[j], 1.0)
                        o_ref[0, c0 + j * V7X_LANES:c0 + (j + 1) * V7X_LANES, :] = t_t.astype(BF16)
                        continue
                    if rope:
                        partner = jnp.where(first_half,
                                            pltpu.roll(t, V7X_LANES - ROPE_FREQS, 1),
                                            pltpu.roll(t, ROPE_FREQS, 1))
                        t = t * cos + partner * sin
                    if kind == "q":
                        t = t * (HEAD_DIM ** -0.5 * LOG2E)
                    tiles.append(t.astype(BF16))
                if tiles:
                    o_ref[0, :, c0:c0 + PROJ_CHUNK] = jnp.concatenate(tiles, axis=-1)
        prev = cur
        yield None if cur is None else cur[0][0:V7X_SUBLANES, 0:V7X_LANES]


def _mixer_in_kernel(*refs, l, outs, rope, conv, r_col, tm, conf_k, cw):
    it = iter(refs)
    if conv:
        xp_ref, x_ref, xn_ref = next(it), next(it), next(it)
    else:
        x_ref = next(it)
    sh_ref, sc_ref, g_ref, w_ref, wkv_ref = next(it), next(it), next(it), next(it), next(it)
    w_refs = (w_ref, wkv_ref)
    layer = slice(l, l + 1)
    cos = sin = None
    if rope:
        cos, sin = next(it)[...], next(it)[...]
    if conv:
        wcf_ref, bcf_ref, lng_ref, lnb_ref, wsc_ref, gc_ref, gs_ref = (next(it) for _ in range(7))
    out_refs = [next(it) for _ in outs]
    if conv:
        zcs_ref, h_ref, u_ref, v_ref = next(it), next(it), next(it), next(it)
    else:
        h_ref = next(it)

    gain = g_ref[layer, :] * (1.0 + sc_ref[0])
    shift = sh_ref[0]

    def modulate(xv):
        return (_rms(xv) * gain + shift).astype(BF16)

    if not conv:
        h_ref[...] = modulate(x_ref[0])
        for _ in _project(h_ref, slice(0, tm), w_refs, out_refs, outs, rope, cos, sin):
            pass
        return

    i = pl.program_id(1)
    top_ok = (i > 0).astype(F32)
    bot_ok = (i < pl.num_programs(1) - 1).astype(F32)
    mid = slice(HALO, HALO + tm)
    top = slice(0, HALO)
    bot = slice(HALO + tm, HALO + tm + HALO)
    h_ref[mid, :] = modulate(x_ref[0])
    h_ref[top, :] = modulate(xp_ref[0])
    h_ref[bot, :] = modulate(xn_ref[0])

    def rdot(k):
        return _dot(h_ref[...], w_ref[0, :, r_col + k * cw:r_col + (k + 1) * cw])

    def store_padded(ref, val):
        ref[mid, :] = val[mid]
        ref[top, :] = val[top] * top_ok
        ref[bot, :] = val[bot] * bot_ok

    cv, cg = rdot(0), rdot(1)
    store_padded(u_ref, cv * jax.nn.sigmoid(cg))

    rc = min(CONV_RC, tm)
    win = rc + 2 * HALO
    pad_c = conf_k // 2

    def conformer_chunk(c, after):
        r0 = c * rc
        zero = None
        if after is not None:
            half = jnp.uint32(16)
            bits = lax.shift_right_logical(lax.shift_right_logical(pltpu.bitcast(after, jnp.uint32), half), half)
            zero = pltpu.bitcast(bits, F32)[0:1, 0:1]
        halves = []
        for hh in range(cw // V7X_LANES):
            lanes = slice(hh * V7X_LANES, (hh + 1) * V7X_LANES)
            wdw = wcf_ref[0, :, lanes]
            window = u_ref[r0:r0 + win, lanes]
            acc = None
            for res in range(V7X_SUBLANES):
                shifted = window if res == 0 else pltpu.roll(window, win - res, 0)
                for k in range(conf_k):
                    off = HALO - pad_c + k
                    if off % V7X_SUBLANES != res:
                        continue
                    term = shifted[off - res:off - res + rc, :] * wdw[k:k + 1, :]
                    acc = term if acc is None else acc + term
            halves.append(acc)
        bias = bcf_ref[layer, :] if zero is None else bcf_ref[layer, :] + zero
        y = jnp.concatenate(halves, axis=-1) + bias
        cen = y - jnp.mean(y, axis=-1, keepdims=True)
        var = jnp.mean(cen * cen, axis=-1, keepdims=True)
        ln = cen * lax.rsqrt(var + EPS) * lng_ref[layer, :] + lnb_ref[layer, :]
        cf = ln * jax.nn.sigmoid(ln)
        zcs_ref[0, r0:r0 + rc, 0:cw] = (_rms(cf) * gc_ref[layer, :]).astype(BF16)

    def short_conv(sb):
        vwin = v_ref[...]
        wsc = wsc_ref[0]
        s = sb[mid] * (pltpu.roll(vwin, 1, 0)[mid] * wsc[0:1, :] + vwin[mid] * wsc[1:2, :]
                       + pltpu.roll(vwin, tm + 2 * HALO - 1, 0)[mid] * wsc[2:3, :])
        zcs_ref[0, :, cw:2 * cw] = (_rms(s) * gs_ref[layer, :]).astype(BF16)

    piece = lambda val: val[0:V7X_SUBLANES, 0:V7X_LANES]
    short_in = {}

    def mxu_work():
        short_in["sb"] = rdot(2)
        yield piece(short_in["sb"])
        scg = rdot(3)
        yield piece(scg)
        su = rdot(4)
        store_padded(v_ref, scg * su)
        yield piece(su)
        yield from _project(h_ref, mid, w_refs, out_refs, outs, rope, cos, sin)

    work = mxu_work()
    pending = None
    for c in range(tm // rc):
        after, pending = pending, next(work, None)
        conformer_chunk(c, after)
    for _ in work:
        pass
    short_conv(short_in["sb"])


def _mixer_in(x, mod, row, per_batch, g_pre, w, w_kv, l, tables, outs, conv_params, tm):
    b, t, d = x.shape
    (w, w_l), (w_kv, w_kv_l) = w, w_kv
    n = w.shape[2]
    rope = tables is not None
    conv = conv_params is not None
    widths = [o[2] for o in outs]
    assert t % tm == 0 and tm % HALO == 0 and all(o[1] + o[2] <= (w, w_kv)[o[0]].shape[2] for o in outs)
    tile = lambda wd: pl.BlockSpec((1, tm, wd), lambda bb, i: (bb, i, 0))
    in_specs, args = [], []
    if conv:
        prev, nxt = _halo_specs(tm, t, d)
        in_specs += [prev, tile(d), nxt]
        args += [x, x, x]
    else:
        in_specs += [tile(d)]
        args += [x]
    in_specs += [_mod_spec(row, 0, d, per_batch), _mod_spec(row, 1, d, per_batch),
                 _rows_spec(g_pre), _layer_spec(w, w_l, single_buffer=True),
                 _layer_spec(w_kv, w_kv_l, single_buffer=True)]
    args += [mod, mod, g_pre, w, w_kv]
    if rope:
        in_specs += [pl.BlockSpec((tm, V7X_LANES), lambda bb, i: (i, 0))] * 2
        args += list(tables)
    is_t = [o[3] in ("v_lo", "v_hi") for o in outs]
    out_shape = [jax.ShapeDtypeStruct((b, wd, t) if tr else (b, t, wd), BF16) for wd, tr in zip(widths, is_t)]
    out_specs = [pl.BlockSpec((1, wd, tm), lambda bb, i: (bb, 0, i)) if tr else tile(wd)
                 for wd, tr in zip(widths, is_t)]
    scratch = [pltpu.VMEM((tm, d), BF16)]
    conf_k = cw = r_col = 0
    if conv:
        w_conf_dw, b_conf_dw, ln_g, ln_b, w_sc_dw, g_group = conv_params
        conf_k, cw = w_conf_dw.shape[1:]
        r_col = n - 5 * cw
        assert w_sc_dw.shape[2] == cw and conf_k // 2 < HALO and g_group.shape[1] == ATTN_W + 2 * cw
        in_specs += [_layer_spec(w_conf_dw, l), _rows_spec(b_conf_dw), _rows_spec(ln_g), _rows_spec(ln_b),
                     _layer_spec(w_sc_dw, l),
                     _rows_spec(g_group, cols=cw, col_block=ATTN_W // cw),
                     _rows_spec(g_group, cols=cw, col_block=ATTN_W // cw + 1)]
        args += [w_conf_dw, b_conf_dw, ln_g, ln_b, w_sc_dw, g_group, g_group]
        out_shape.append(jax.ShapeDtypeStruct((b, t, 2 * cw), BF16))
        out_specs.append(tile(2 * cw))
        rows = tm + 2 * HALO
        scratch = [pltpu.VMEM((rows, d), BF16), pltpu.VMEM((rows, cw), F32), pltpu.VMEM((rows, cw), F32)]
    kern = functools.partial(_mixer_in_kernel, l=l, outs=tuple(outs), rope=rope, conv=conv, r_col=r_col,
                             tm=tm, conf_k=conf_k, cw=cw)
    return pl.pallas_call(
        kern,
        out_shape=out_shape,
        grid=(b, t // tm),
        in_specs=in_specs,
        out_specs=out_specs,
        scratch_shapes=scratch,
        compiler_params=_params("parallel", "parallel"),
        name="mixer_in",
    )(*args)


def _mix_kernel(*refs, band, t, tq, l, n_cast):
    refs = list(refs)
    if n_cast:
        cast_out = refs[-n_cast:]
        cast_in = refs[-2 - 2 * n_cast:-2 - n_cast]
        refs = refs[:-2 - 2 * n_cast] + refs[-2 - n_cast:-n_cast]
        _cast_rows(cast_in, cast_out)
    if band:
        (sink_ref, q_ref, kk_ref, vlo_ref, vhi_ref, kc_ref, vclo_ref, vchi_ref, zcs_ref, x_ref,
         gt_ref, sc2_ref, sh2_ref, ga_ref, gpost_ref, gffn_ref, wout_ref, xo_ref, h2_ref) = refs
    else:
        (sink_ref, q_ref, kc_ref, vclo_ref, vchi_ref, zcs_ref, x_ref,
         gt_ref, sc2_ref, sh2_ref, ga_ref, gpost_ref, gffn_ref, wout_ref, xo_ref, h2_ref) = refs
    blk = ATTN_BLOCK
    band_w = blk + 2 * WINDOW
    lane = lax.broadcasted_iota(jnp.int32, (1, V7X_LANES), 1)
    lo_lanes = lane < HEAD_DIM
    m_lo = jnp.where(lo_lanes, 1.0, 0.0).astype(BF16)
    m_hi = jnp.where(lo_lanes, 0.0, 1.0).astype(BF16)
    t0 = pl.program_id(1) * tq
    n_blocks = tq // blk
    groups = [(qb, h) for qb in range(n_blocks) for h in range(N_KV_HEADS)]

    def window(qb):
        q0 = t0 + qb * blk
        return q0, pl.multiple_of(jnp.clip(q0 - WINDOW, 0, t - band_w), blk)

    def scores(qb, h):
        rows = slice(qb * blk, (qb + 1) * blk)
        kv_lanes = slice(h * V7X_LANES, (h + 1) * V7X_LANES)
        qp0 = q_ref[0, rows, (2 * h) * V7X_LANES:(2 * h + 1) * V7X_LANES]
        qp1 = q_ref[0, rows, (2 * h + 1) * V7X_LANES:(2 * h + 2) * V7X_LANES]
        qs = jnp.concatenate([qp0 * m_lo, qp1 * m_lo, qp0 * m_hi, qp1 * m_hi], axis=0)
        s_ctx = _dot_nt(kc_ref[0, :, kv_lanes], qs)
        s_band = None
        if band:
            _, start = window(qb)
            s_band = _dot_nt(kk_ref[0, pl.ds(start, band_w), kv_lanes], qs)
        return s_ctx, s_band

    hidden = {}
    dims_first = lax.broadcasted_iota(jnp.int32, (V7X_LANES, 1), 0) < HEAD_DIM

    def attend(qb, h, s_ctx, s_band):
        kv_rows = slice(h * V7X_LANES, (h + 1) * V7X_LANES)
        heads = (4 * h, 4 * h + 2, 4 * h + 1, 4 * h + 3)
        if band:
            q0, start = window(qb)
            if qb not in hidden:
                rel = (lax.broadcasted_iota(jnp.int32, (band_w, blk), 0)
                       - lax.broadcasted_iota(jnp.int32, (band_w, blk), 1)) + (start - q0)
                hidden[qb] = jnp.where(jnp.abs(rel) <= WINDOW, 0.0, NEG)
        p_ctx, p_band, snk = [], [], []
        for r, head in enumerate(heads):
            cc = slice(r * blk, (r + 1) * blk)
            sink2 = sink_ref[l * N_Q_HEADS + head] * LOG2E
            sc = s_ctx[:, cc]
            m = jnp.maximum(jnp.max(sc, axis=0, keepdims=True), sink2)
            if band:
                sb = s_band[:, cc] + hidden[qb]
                m = jnp.maximum(m, jnp.max(sb, axis=0, keepdims=True))
                p_band.append(jnp.exp2(sb - m).astype(BF16))
            p_ctx.append(jnp.exp2(sc - m).astype(BF16))
            snk.append(jnp.exp2(sink2 - m))
        o_lo = _dot(vclo_ref[0, kv_rows, :], jnp.concatenate(p_ctx[0:2], axis=1))
        o_hi = _dot(vchi_ref[0, kv_rows, :], jnp.concatenate(p_ctx[2:4], axis=1))
        if band:
            o_lo = o_lo + _dot(vlo_ref[0, kv_rows, pl.ds(start, band_w)], jnp.concatenate(p_band[0:2], axis=1))
            o_hi = o_hi + _dot(vhi_ref[0, kv_rows, pl.ds(start, band_w)], jnp.concatenate(p_band[2:4], axis=1))
        num = jnp.where(dims_first, o_lo, o_hi)
        den = jnp.where(dims_first,
                        pltpu.roll(o_lo, HEAD_DIM, 0) + jnp.concatenate(snk[0:2], axis=1),
                        pltpu.roll(o_hi, HEAD_DIM, 0) + jnp.concatenate(snk[2:4], axis=1))
        o_t = num / den
        return [o_t[:, 0:blk].T, o_t[:, blk:2 * blk].T]

    merge_rows = min(tq, MIX_MERGE_ROWS)
    blocks_per_merge = merge_rows // blk
    layer = slice(l, l + 1)
    post_gain = gt_ref[0] * gpost_ref[layer, :]
    ffn_gain = gffn_ref[layer, :] * (1.0 + sc2_ref[0])

    def merge(mi, tiles):
        rows = slice(mi * merge_rows, (mi + 1) * merge_rows)
        a = jnp.concatenate(
            [jnp.concatenate([tl for h in range(N_KV_HEADS) for tl in tiles[(qb, h)]], axis=-1)
             for qb in range(mi * blocks_per_merge, (mi + 1) * blocks_per_merge)], axis=0)
        za = (_rms(a) * ga_ref[layer, :]).astype(BF16)
        y = _dot(za, wout_ref[0, 0:ATTN_W, :]) + _dot(zcs_ref[0, rows, :], wout_ref[0, ATTN_W:, :])
        x_mid = x_ref[0, rows, :] + _rms(y) * post_gain
        xo_ref[0, rows, :] = x_mid
        h2_ref[0, rows, :] = (_rms(x_mid) * ffn_gain + sh2_ref[0]).astype(BF16)

    tiles = {}
    nxt = scores(*groups[0])
    for gi, (qb, h) in enumerate(groups):
        cur = nxt
        if gi + 1 < len(groups):
            nxt = scores(*groups[gi + 1])
        tiles[(qb, h)] = attend(qb, h, *cur)
        if h == N_KV_HEADS - 1 and (qb + 1) % blocks_per_merge == 0:
            merge(qb // blocks_per_merge, tiles)


def _mix(sink, l, q, kk, vlo, vhi, kc, vclo, vchi, zcs, x, mod, row, per_batch,
         g_group, g_post, g_ffn, w_out, tq, casts=()):
    b, t, d = x.shape
    band = kk is not None
    ctx_len = kc.shape[1]
    assert t % tq == 0 and tq % min(tq, MIX_MERGE_ROWS) == 0 and MIX_MERGE_ROWS % ATTN_BLOCK == 0
    tile = lambda w: pl.BlockSpec((1, tq, w), lambda bb, i: (bb, i, 0))
    whole = lambda n, w: pl.BlockSpec((1, n, w), lambda bb, i: (bb, 0, 0))
    in_specs = [pl.BlockSpec(memory_space=pltpu.SMEM), tile(ATTN_W)]
    args = [sink, q]
    whole_t = lambda n: pl.BlockSpec((1, KV_DUP_W, n), lambda bb, i: (bb, 0, 0))
    if band:
        in_specs += [whole(t, KV_DUP_W), whole_t(t), whole_t(t)]
        args += [kk, vlo, vhi]
    in_specs += [whole(ctx_len, KV_DUP_W), whole_t(ctx_len), whole_t(ctx_len)]
    in_specs += [tile(zcs.shape[2]), tile(d),
                 _mod_spec(row, 2, d, per_batch), _mod_spec(row, 4, d, per_batch),
                 _mod_spec(row, 3, d, per_batch),
                 _rows_spec(g_group, cols=ATTN_W), _rows_spec(g_post), _rows_spec(g_ffn),
                 _layer_spec(w_out[0], w_out[1])]
    args += [kc, vclo, vchi, zcs, x, mod, mod, mod, g_group, g_post, g_ffn, w_out[0]]
    grid = (b, t // tq)
    c_in, c_args, c_shapes, c_out = _cast_slots(casts, grid)
    kern = functools.partial(_mix_kernel, band=band, t=t, tq=tq, l=l, n_cast=len(casts))
    return pl.pallas_call(
        kern,
        out_shape=[jax.ShapeDtypeStruct((b, t, d), F32), jax.ShapeDtypeStruct((b, t, d), BF16)] + c_shapes,
        grid=grid,
        in_specs=in_specs + c_in,
        out_specs=[tile(d), tile(d)] + c_out,
        compiler_params=_params("arbitrary", "arbitrary") if casts else _params("parallel", "parallel"),
        name="mix",
    )(*args, *c_args)


def _ffn_kernel(*refs, l, tm, d_ff, tf, n_cast):
    hp_ref, h_ref, hn_ref, x_ref, gt_ref, gpost_ref, wup_ref, wdw_ref, wdn_ref = refs[:9]
    cast_in, o_ref, cast_out = refs[9:9 + n_cast], refs[9 + n_cast], refs[10 + n_cast:10 + 2 * n_cast]
    hcat_ref, act_ref = refs[10 + 2 * n_cast:]
    _cast_rows(cast_in, cast_out)
    i = pl.program_id(1)
    rows = tm + 2 * HALO
    hcat_ref[HALO:HALO + tm, :] = h_ref[0]

    @pl.when(i > 0)
    def _():
        hcat_ref[0:HALO, :] = hp_ref[0]

    @pl.when(i == 0)
    def _():
        hcat_ref[0:HALO, :] = jnp.zeros((HALO, hcat_ref.shape[1]), BF16)

    @pl.when(i < pl.num_programs(1) - 1)
    def _():
        hcat_ref[HALO + tm:, :] = hn_ref[0]

    @pl.when(i == pl.num_programs(1) - 1)
    def _():
        hcat_ref[HALO + tm:, :] = jnp.zeros((HALO, hcat_ref.shape[1]), BF16)

    hc = hcat_ref[...]
    mid = slice(HALO, HALO + tm)

    def conv3(u, w):
        return (pltpu.roll(u, 1, 0)[mid] * w[0:1, :] + u[mid] * w[1:2, :]
                + pltpu.roll(u, rows - 1, 0)[mid] * w[2:3, :])

    n_chunks = d_ff // tf

    def up(c):
        gcols = slice(c * tf, (c + 1) * tf)
        vcols = slice(d_ff + c * tf, d_ff + (c + 1) * tf)
        return _dot(hc, wup_ref[0, :, gcols]), _dot(hc, wup_ref[0, :, vcols])

    nxt = up(0)
    for c in range(n_chunks):
        ug, uv = nxt
        if c + 1 < n_chunks:
            nxt = up(c + 1)
        gcols = slice(c * tf, (c + 1) * tf)
        vcols = slice(d_ff + c * tf, d_ff + (c + 1) * tf)
        gate = conv3(ug, wdw_ref[0, :, gcols])
        val = conv3(uv, wdw_ref[0, :, vcols])
        act_ref[:, gcols] = (gate * jax.nn.sigmoid(gate) * val).astype(BF16)
    y = _dot(act_ref[...], wdn_ref[0])
    o_ref[0] = x_ref[0] + (gt_ref[0] * gpost_ref[l:l + 1, :]) * _rms(y)


def _ffn(h2, x, mod, row, per_batch, l, g_post, w_up, w_dw, w_down, tm, casts=()):
    b, t, d = x.shape
    (w_up, w_up_l), (w_down, w_down_l) = w_up, w_down
    d_ff = w_down.shape[1]
    assert t % tm == 0 and tm % HALO == 0 and d_ff % FFN_TF == 0
    prev, nxt = _halo_specs(tm, t, d)
    tile = pl.BlockSpec((1, tm, d), lambda bb, i: (bb, i, 0))
    grid = (b, t // tm)
    c_in, c_args, c_shapes, c_out = _cast_slots(casts, grid)
    kern = functools.partial(_ffn_kernel, l=l, tm=tm, d_ff=d_ff, tf=FFN_TF, n_cast=len(casts))
    out = pl.pallas_call(
        kern,
        out_shape=[jax.ShapeDtypeStruct((b, t, d), F32)] + c_shapes,
        grid=grid,
        in_specs=[prev, tile, nxt, tile, _mod_spec(row, 5, d, per_batch), _rows_spec(g_post),
                  _layer_spec(w_up, w_up_l, single_buffer=True), _layer_spec(w_dw, l),
                  _layer_spec(w_down, w_down_l, single_buffer=True)] + c_in,
        out_specs=[tile] + c_out,
        scratch_shapes=[pltpu.VMEM((tm + 2 * HALO, d), BF16), pltpu.VMEM((tm, d_ff), BF16)],
        compiler_params=_params("arbitrary", "arbitrary") if casts else _params("parallel", "parallel"),
        name="conv_ffn",
    )(h2, h2, h2, x, mod, g_post, w_up, w_dw, w_down, *c_args)
    return out if casts else out[0]


def _rope_tables(n_tokens):
    lane = np.arange(V7X_LANES)
    row_axis = ((lane % HEAD_DIM) // (2 * ROPE_FREQS)) == 0
    sign = np.where((lane % (2 * ROPE_FREQS)) < ROPE_FREQS, -1.0, 1.0).astype(np.float32)
    inv = ROPE_THETA ** (-jnp.arange(ROPE_FREQS, dtype=F32) / ROPE_FREQS)
    tok = jnp.arange(n_tokens)
    pos = jnp.where(jnp.asarray(row_axis)[None, :], (tok // GRID_W)[:, None], (tok % GRID_W)[:, None])
    ang = pos.astype(F32) * jnp.tile(inv, V7X_LANES // ROPE_FREQS)[None, :]
    return jnp.cos(ang), jnp.sin(ang) * sign[None, :]


def _kv_dup_weights(w_in):
    depth, d, _ = w_in.shape
    heads = w_in[:, :, ATTN_W:ATTN_W + 2 * KV_W].astype(BF16).reshape(depth, d, 2 * N_KV_HEADS, 1, HEAD_DIM)
    return jnp.broadcast_to(heads, (depth, d, 2 * N_KV_HEADS, 2, HEAD_DIM)).reshape(depth, d, 2 * KV_DUP_W)


def kernel(x, c, ctx, c_ctx, w_ada, b_ada, g_pre_mix, g_post_mix, g_pre_ffn, g_post_ffn, w_in, sink,
           w_conf_dw, b_conf_dw, conf_ln_g, conf_ln_b, w_sc_dw, g_group, w_out, w_up, w_ffn_dw, w_down):
    batch, seq, d = x.shape
    depth = w_in.shape[0]
    ctx_len = ctx.shape[1]
    assert batch < MOD_ROWS

    cc = jnp.concatenate([c, c_ctx[None, :], jnp.zeros((MOD_ROWS - batch - 1, d), F32)], axis=0)
    mod = _modulation(cc, w_ada, b_ada).reshape(depth * MOD_ROWS, 1, N_MOD * d)
    tables = _rope_tables(seq)
    sink_flat = sink.reshape(-1).astype(F32)
    w_kv = _kv_dup_weights(w_in)
    conv_params = (w_conf_dw, b_conf_dw, conf_ln_g, conf_ln_b, w_sc_dw, g_group)

    kv_outs = [(1, 0, KV_DUP_W, "k"), (1, KV_DUP_W, KV_DUP_W, "v_lo"), (1, KV_DUP_W, KV_DUP_W, "v_hi")]
    all_outs = [(0, 0, ATTN_W, "q")] + kv_outs

    w_in_b, w_out_b = (w_in[0:1].astype(BF16), 0), (w_out[0:1].astype(BF16), 0)

    for l in range(depth):
        update_ctx = l < depth - 1
        row = l * MOD_ROWS
        q, kk, vlo, vhi, zcs = _mixer_in(x, mod, row, True, g_pre_mix, w_in_b, (w_kv, l), l, tables, all_outs,
                                         conv_params, IN_TM)
        if update_ctx:
            qc, kc, vclo, vchi, zcs_c = _mixer_in(ctx, mod, row + batch, False, g_pre_mix, w_in_b, (w_kv, l), l,
                                                  None, all_outs, conv_params, ctx_len)
        else:
            kc, vclo, vchi = _mixer_in(ctx, mod, row + batch, False, g_pre_mix, w_in_b, (w_kv, l), l, None,
                                       kv_outs, None, ctx_len)
        x_mid, h2, w_up_l, w_down_l = _mix(sink_flat, l, q, kk, vlo, vhi, kc, vclo, vchi, zcs, x, mod, row, True,
                                           g_group, g_post_mix, g_pre_ffn, w_out_b, MIX_TQ,
                                           casts=((w_up, l), (w_down, l)))
        w_up_b, w_down_b = (w_up_l, 0), (w_down_l, 0)
        if update_ctx:
            ctx_mid, hc2 = _mix(sink_flat, l, qc, None, None, None, kc, vclo, vchi, zcs_c, ctx, mod,
                                row + batch, False, g_group, g_post_mix, g_pre_ffn, w_out_b, ctx_len)
            ctx = _ffn(hc2, ctx_mid, mod, row + batch, False, l, g_post_ffn, w_up_b, w_ffn_dw, w_down_b,
                       ctx_len)
        if l + 1 < depth:
            x, w_in_next, w_out_next = _ffn(h2, x_mid, mod, row, True, l, g_post_ffn, w_up_b, w_ffn_dw, w_down_b,
                                            FFN_TM, casts=((w_in, l + 1), (w_out, l + 1)))
            w_in_b, w_out_b = (w_in_next, 0), (w_out_next, 0)
        else:
            x = _ffn(h2, x_mid, mod, row, True, l, g_post_ffn, w_up_b, w_ffn_dw, w_down_b, FFN_TM)
    return x
```

```python
import functools

import numpy as np
import jax
import jax.numpy as jnp
from jax import lax
from jax.experimental import pallas as pl
from jax.experimental.pallas import tpu as pltpu

F32 = jnp.float32
BF16 = jnp.bfloat16

V7X_LANES = 128
V7X_SUBLANES = 8
V7X_BF16_SUBLANES = 16
V7X_VMEM_LIMIT_BYTES = 56 * 1024 * 1024

HEAD_DIM = 64
N_Q_HEADS = 8
N_KV_HEADS = 2
GRID_W = 64
WINDOW = 128
ATTN_BLOCK = 128
ROPE_FREQS = HEAD_DIM // 4
ROPE_THETA = 10000.0
EPS = 1e-6
NEG = -1e30
LOG2E = 1.4426950408889634

ATTN_W = N_Q_HEADS * HEAD_DIM
KV_W = N_KV_HEADS * HEAD_DIM
KV_DUP_W = 2 * KV_W
N_MOD = 6
MOD_ROWS = 16

IN_TM = 512
MIX_TQ = 512
MIX_MERGE_ROWS = 256
MIX_AHEAD = 2
FFN_TM = 512
FFN_TF = 256
HALO = V7X_BF16_SUBLANES
CONV_RC = 64
PROJ_CHUNK = 2 * V7X_LANES
MOD_TN = 1536


def _rms(x, eps=EPS):
    return x * lax.rsqrt(jnp.mean(x * x, axis=-1, keepdims=True) + eps)


def _dot(a, b):
    return jnp.dot(a, b, preferred_element_type=F32)


def _dot_nt(a, b):
    return lax.dot_general(a, b, (((1,), (1,)), ((), ())), preferred_element_type=F32)


def _params(*sem):
    return pltpu.CompilerParams(dimension_semantics=sem, vmem_limit_bytes=V7X_VMEM_LIMIT_BYTES)


def _layer_spec(arr, l, cols=None, col_block=0, single_buffer=False):
    _, rows, width = arr.shape
    width = cols or width
    mode = dict(pipeline_mode=pl.Buffered(1)) if single_buffer else {}
    return pl.BlockSpec((1, rows, width), lambda *_: (l, 0, col_block), **mode)


def _rows_spec(arr, cols=None, col_block=0):
    depth, width = arr.shape
    return pl.BlockSpec((depth, cols or width), lambda *_: (0, col_block))


def _mod_spec(row, k, d, per_batch):
    if per_batch:
        return pl.BlockSpec((1, 1, d), lambda b, i: (row + b, 0, k))
    return pl.BlockSpec((1, 1, d), lambda b, i: (row, 0, k))


def _cast_slots(weights, grid):
    steps = grid[0] * grid[1]
    in_specs, args, out_shapes, out_specs = [], [], [], []
    for arr, layer in weights:
        _, r, c = arr.shape
        span = 1
        while r % (steps // span) or (r // (steps // span)) % V7X_BF16_SUBLANES:
            span *= 2
            assert span <= steps, (r, steps)
        rows = r // (steps // span)
        in_specs.append(pl.BlockSpec((1, rows, c), lambda bb, i, layer=layer, span=span:
                                     (layer, (bb * grid[1] + i) // span, 0)))
        out_specs.append(pl.BlockSpec((1, rows, c), lambda bb, i, span=span: (0, (bb * grid[1] + i) // span, 0)))
        out_shapes.append(jax.ShapeDtypeStruct((1, r, c), BF16))
        args.append(arr)
    return in_specs, args, out_shapes, out_specs


def _cast_rows(cast_in, cast_out):
    for src, dst in zip(cast_in, cast_out):
        dst[...] = src[...].astype(BF16)


def _halo_specs(tm, t, d):
    per_tile = tm // HALO
    last = t // HALO - 1
    return (pl.BlockSpec((1, HALO, d), lambda bb, i: (bb, jnp.maximum(i * per_tile - 1, 0), 0)),
            pl.BlockSpec((1, HALO, d), lambda bb, i: (bb, jnp.minimum((i + 1) * per_tile, last), 0)))


def _mod_kernel(*refs, n_cast):
    c_ref, w_ref, b_ref = refs[:3]
    cast_in, o_ref, cast_out = refs[3:3 + n_cast], refs[3 + n_cast], refs[4 + n_cast:]
    _cast_rows(cast_in, cast_out)
    c = c_ref[...]
    a = c * jax.nn.sigmoid(c)
    w = w_ref[0]
    a_hi = a.astype(BF16)
    a_lo = (a - a_hi.astype(F32)).astype(BF16)
    w_hi = w.astype(BF16)
    w_lo = (w - w_hi.astype(F32)).astype(BF16)
    hi_lo = _dot(jnp.concatenate([a_hi, a_lo], axis=0), w_hi)
    acc = hi_lo[0:MOD_ROWS] + hi_lo[MOD_ROWS:2 * MOD_ROWS] + _dot(a_hi, w_lo)
    o_ref[0] = acc + b_ref[0]


def _modulation(cc, w_ada, b_ada, casts):
    depth, d, n = w_ada.shape
    grid = (depth, n // MOD_TN)
    c_in, c_args, c_shapes, c_out = _cast_slots(casts, grid)
    return pl.pallas_call(
        functools.partial(_mod_kernel, n_cast=len(casts)),
        out_shape=[jax.ShapeDtypeStruct((depth, MOD_ROWS, n), F32)] + c_shapes,
        grid=grid,
        in_specs=[
            pl.BlockSpec((MOD_ROWS, d), lambda l, j: (0, 0)),
            pl.BlockSpec((1, d, MOD_TN), lambda l, j: (l, 0, j)),
            pl.BlockSpec((1, 1, MOD_TN), lambda l, j: (l, 0, j)),
        ] + c_in,
        out_specs=[pl.BlockSpec((1, MOD_ROWS, MOD_TN), lambda l, j: (l, 0, j))] + c_out,
        compiler_params=_params("arbitrary", "arbitrary"),
        name="modulation",
    )(cc, w_ada, b_ada.reshape(depth, 1, n), *c_args)


def _project(h_ref, rows, w_refs, out_refs, outs, rope, cos, sin):
    lane = lax.broadcasted_iota(jnp.int32, (1, V7X_LANES), 1)
    dims_first = lax.broadcasted_iota(jnp.int32, (V7X_LANES, 1), 0) < HEAD_DIM
    first_half = (lane % (2 * ROPE_FREQS)) < ROPE_FREQS
    chunks = {}
    for o_ref, (wi, col, width, kind) in zip(out_refs, outs):
        for c0 in range(0, width, PROJ_CHUNK):
            chunks.setdefault((wi, col + c0, col + c0 + PROJ_CHUNK), []).append((o_ref, c0, kind))
    prev = None
    for cols, users in list(chunks.items()) + [(None, None)]:
        cur = (_dot(h_ref[rows, :], w_refs[cols[0]][0, :, cols[1]:cols[2]]), users) if users else None
        if prev is not None:
            p, p_users = prev
            transposed = {}
            for o_ref, c0, kind in p_users:
                tiles = []
                for j in range(PROJ_CHUNK // V7X_LANES):
                    t = p[:, j * V7X_LANES:(j + 1) * V7X_LANES]
                    if kind in ("v_lo", "v_hi"):
                        if j not in transposed:
                            transposed[j] = t.T
                        keep = dims_first if kind == "v_lo" else jnp.logical_not(dims_first)
                        t_t = jnp.where(keep, transposed[j], 1.0)
                        o_ref[0, c0 + j * V7X_LANES:c0 + (j + 1) * V7X_LANES, :] = t_t.astype(BF16)
                        continue
                    if rope:
                        partner = jnp.where(first_half,
                                            pltpu.roll(t, V7X_LANES - ROPE_FREQS, 1),
                                            pltpu.roll(t, ROPE_FREQS, 1))
                        t = t * cos + partner * sin
                    if kind == "q":
                        t = t * (HEAD_DIM ** -0.5 * LOG2E)
                    tiles.append(t.astype(BF16))
                if tiles:
                    o_ref[0, :, c0:c0 + PROJ_CHUNK] = jnp.concatenate(tiles, axis=-1)
        prev = cur
        yield None if cur is None else cur[0][0:V7X_SUBLANES, 0:V7X_LANES]


def _mixer_in_kernel(*refs, l, outs, rope, conv, r_col, tm, conf_k, cw):
    it = iter(refs)
    if conv:
        xp_ref, x_ref, xn_ref = next(it), next(it), next(it)
    else:
        x_ref = next(it)
    sh_ref, sc_ref, g_ref, w_ref, wkv_ref = next(it), next(it), next(it), next(it), next(it)
    w_refs = (w_ref, wkv_ref)
    layer = slice(l, l + 1)
    cos = sin = None
    if rope:
        cos, sin = next(it)[...], next(it)[...]
    if conv:
        wcf_ref, bcf_ref, lng_ref, lnb_ref, wsc_ref, gc_ref, gs_ref = (next(it) for _ in range(7))
    out_refs = [next(it) for _ in outs]
    if conv:
        zcs_ref, h_ref, u_ref, v_ref = next(it), next(it), next(it), next(it)
    else:
        h_ref = next(it)

    gain = g_ref[layer, :] * (1.0 + sc_ref[0])
    shift = sh_ref[0]

    def modulate(xv):
        return (_rms(xv) * gain + shift).astype(BF16)

    if not conv:
        h_ref[...] = modulate(x_ref[0])
        for _ in _project(h_ref, slice(0, tm), w_refs, out_refs, outs, rope, cos, sin):
            pass
        return

    i = pl.program_id(1)
    top_ok = (i > 0).astype(F32)
    bot_ok = (i < pl.num_programs(1) - 1).astype(F32)
    mid = slice(HALO, HALO + tm)
    top = slice(0, HALO)
    bot = slice(HALO + tm, HALO + tm + HALO)
    h_ref[mid, :] = modulate(x_ref[0])
    h_ref[top, :] = modulate(xp_ref[0])
    h_ref[bot, :] = modulate(xn_ref[0])

    def rdot(k):
        return _dot(h_ref[...], w_ref[0, :, r_col + k * cw:r_col + (k + 1) * cw])

    def store_padded(ref, val):
        ref[mid, :] = val[mid]
        ref[top, :] = val[top] * top_ok
        ref[bot, :] = val[bot] * bot_ok

    cv, cg = rdot(0), rdot(1)
    store_padded(u_ref, cv * jax.nn.sigmoid(cg))

    rc = min(CONV_RC, tm)
    win = rc + 2 * HALO
    pad_c = conf_k // 2

    def conformer_chunk(c, after):
        r0 = c * rc
        zero = None
        if after is not None:
            half = jnp.uint32(16)
            bits = lax.shift_right_logical(lax.shift_right_logical(pltpu.bitcast(after, jnp.uint32), half), half)
            zero = pltpu.bitcast(bits, F32)[0:1, 0:1]
        halves = []
        for hh in range(cw // V7X_LANES):
            lanes = slice(hh * V7X_LANES, (hh + 1) * V7X_LANES)
            wdw = wcf_ref[0, :, lanes]
            window = u_ref[r0:r0 + win, lanes]
            acc = None
            for res in range(V7X_SUBLANES):
                shifted = window if res == 0 else pltpu.roll(window, win - res, 0)
                for k in range(conf_k):
                    off = HALO - pad_c + k
                    if off % V7X_SUBLANES != res:
                        continue
                    term = shifted[off - res:off - res + rc, :] * wdw[k:k + 1, :]
                    acc = term if acc is None else acc + term
            halves.append(acc)
        bias = bcf_ref[layer, :] if zero is None else bcf_ref[layer, :] + zero
        y = jnp.concatenate(halves, axis=-1) + bias
        cen = y - jnp.mean(y, axis=-1, keepdims=True)
        var = jnp.mean(cen * cen, axis=-1, keepdims=True)
        ln = cen * lax.rsqrt(var + EPS) * lng_ref[layer, :] + lnb_ref[layer, :]
        cf = ln * jax.nn.sigmoid(ln)
        zcs_ref[0, r0:r0 + rc, 0:cw] = (_rms(cf) * gc_ref[layer, :]).astype(BF16)

    def short_conv(sb):
        vwin = v_ref[...]
        wsc = wsc_ref[0]
        s = sb[mid] * (pltpu.roll(vwin, 1, 0)[mid] * wsc[0:1, :] + vwin[mid] * wsc[1:2, :]
                       + pltpu.roll(vwin, tm + 2 * HALO - 1, 0)[mid] * wsc[2:3, :])
        zcs_ref[0, :, cw:2 * cw] = (_rms(s) * gs_ref[layer, :]).astype(BF16)

    piece = lambda val: val[0:V7X_SUBLANES, 0:V7X_LANES]
    short_in = {}

    def mxu_work():
        short_in["sb"] = rdot(2)
        yield piece(short_in["sb"])
        scg = rdot(3)
        yield piece(scg)
        su = rdot(4)
        store_padded(v_ref, scg * su)
        yield piece(su)
        yield from _project(h_ref, mid, w_refs, out_refs, outs, rope, cos, sin)

    work = mxu_work()
    pending = None
    for c in range(tm // rc):
        after, pending = pending, next(work, None)
        conformer_chunk(c, after)
    for _ in work:
        pass
    short_conv(short_in["sb"])


def _mixer_in(x, mod, row, per_batch, g_pre, w, w_kv, l, tables, outs, conv_params, tm):
    b, t, d = x.shape
    (w, w_l), (w_kv, w_kv_l) = w, w_kv
    n = w.shape[2]
    rope = tables is not None
    conv = conv_params is not None
    widths = [o[2] for o in outs]
    assert t % tm == 0 and tm % HALO == 0 and all(o[1] + o[2] <= (w, w_kv)[o[0]].shape[2] for o in outs)
    tile = lambda wd: pl.BlockSpec((1, tm, wd), lambda bb, i: (bb, i, 0))
    in_specs, args = [], []
    if conv:
        prev, nxt = _halo_specs(tm, t, d)
        in_specs += [prev, tile(d), nxt]
        args += [x, x, x]
    else:
        in_specs += [tile(d)]
        args += [x]
    in_specs += [_mod_spec(row, 0, d, per_batch), _mod_spec(row, 1, d, per_batch),
                 _rows_spec(g_pre), _layer_spec(w, w_l, single_buffer=True),
                 _layer_spec(w_kv, w_kv_l, single_buffer=True)]
    args += [mod, mod, g_pre, w, w_kv]
    if rope:
        in_specs += [pl.BlockSpec((tm, V7X_LANES), lambda bb, i: (i, 0))] * 2
        args += list(tables)
    is_t = [o[3] in ("v_lo", "v_hi") for o in outs]
    out_shape = [jax.ShapeDtypeStruct((b, wd, t) if tr else (b, t, wd), BF16) for wd, tr in zip(widths, is_t)]
    out_specs = [pl.BlockSpec((1, wd, tm), lambda bb, i: (bb, 0, i)) if tr else tile(wd)
                 for wd, tr in zip(widths, is_t)]
    scratch = [pltpu.VMEM((tm, d), BF16)]
    conf_k = cw = r_col = 0
    if conv:
        w_conf_dw, b_conf_dw, ln_g, ln_b, w_sc_dw, g_group = conv_params
        conf_k, cw = w_conf_dw.shape[1:]
        r_col = n - 5 * cw
        assert w_sc_dw.shape[2] == cw and conf_k // 2 < HALO and g_group.shape[1] == ATTN_W + 2 * cw
        in_specs += [_layer_spec(w_conf_dw, l), _rows_spec(b_conf_dw), _rows_spec(ln_g), _rows_spec(ln_b),
                     _layer_spec(w_sc_dw, l),
                     _rows_spec(g_group, cols=cw, col_block=ATTN_W // cw),
                     _rows_spec(g_group, cols=cw, col_block=ATTN_W // cw + 1)]
        args += [w_conf_dw, b_conf_dw, ln_g, ln_b, w_sc_dw, g_group, g_group]
        out_shape.append(jax.ShapeDtypeStruct((b, t, 2 * cw), BF16))
        out_specs.append(tile(2 * cw))
        rows = tm + 2 * HALO
        scratch = [pltpu.VMEM((rows, d), BF16), pltpu.VMEM((rows, cw), F32), pltpu.VMEM((rows, cw), F32)]
    kern = functools.partial(_mixer_in_kernel, l=l, outs=tuple(outs), rope=rope, conv=conv, r_col=r_col,
                             tm=tm, conf_k=conf_k, cw=cw)
    return pl.pallas_call(
        kern,
        out_shape=out_shape,
        grid=(b, t // tm),
        in_specs=in_specs,
        out_specs=out_specs,
        scratch_shapes=scratch,
        compiler_params=_params("parallel", "parallel"),
        name="mixer_in",
    )(*args)


def _mix_kernel(*refs, band, t, tq, l, n_cast):
    refs = list(refs)
    if n_cast:
        cast_out = refs[-n_cast:]
        cast_in = refs[-2 - 2 * n_cast:-2 - n_cast]
        refs = refs[:-2 - 2 * n_cast] + refs[-2 - n_cast:-n_cast]
        _cast_rows(cast_in, cast_out)
    if band:
        (sink_ref, q_ref, kk_ref, vlo_ref, vhi_ref, kc_ref, vclo_ref, vchi_ref, zcs_ref, x_ref,
         gt_ref, sc2_ref, sh2_ref, ga_ref, gpost_ref, gffn_ref, wout_ref, xo_ref, h2_ref) = refs
    else:
        (sink_ref, q_ref, kc_ref, vclo_ref, vchi_ref, zcs_ref, x_ref,
         gt_ref, sc2_ref, sh2_ref, ga_ref, gpost_ref, gffn_ref, wout_ref, xo_ref, h2_ref) = refs
    blk = ATTN_BLOCK
    band_w = blk + 2 * WINDOW
    lane = lax.broadcasted_iota(jnp.int32, (1, V7X_LANES), 1)
    lo_lanes = lane < HEAD_DIM
    m_lo = jnp.where(lo_lanes, 1.0, 0.0).astype(BF16)
    m_hi = jnp.where(lo_lanes, 0.0, 1.0).astype(BF16)
    t0 = pl.program_id(1) * tq
    n_blocks = tq // blk
    groups = [(qb, h) for qb in range(n_blocks) for h in range(N_KV_HEADS)]

    def window(qb):
        q0 = t0 + qb * blk
        return q0, pl.multiple_of(jnp.clip(q0 - WINDOW, 0, t - band_w), blk)

    def scores(qb, h):
        rows = slice(qb * blk, (qb + 1) * blk)
        kv_lanes = slice(h * V7X_LANES, (h + 1) * V7X_LANES)
        qp0 = q_ref[0, rows, (2 * h) * V7X_LANES:(2 * h + 1) * V7X_LANES]
        qp1 = q_ref[0, rows, (2 * h + 1) * V7X_LANES:(2 * h + 2) * V7X_LANES]
        qs = jnp.concatenate([qp0 * m_lo, qp1 * m_lo, qp0 * m_hi, qp1 * m_hi], axis=0)
        s_ctx = _dot_nt(kc_ref[0, :, kv_lanes], qs)
        s_band = None
        if band:
            _, start = window(qb)
            s_band = _dot_nt(kk_ref[0, pl.ds(start, band_w), kv_lanes], qs)
        return s_ctx, s_band

    hidden = {}
    dims_first = lax.broadcasted_iota(jnp.int32, (V7X_LANES, 1), 0) < HEAD_DIM

    def attend(qb, h, s_ctx, s_band):
        kv_rows = slice(h * V7X_LANES, (h + 1) * V7X_LANES)
        heads = (4 * h, 4 * h + 2, 4 * h + 1, 4 * h + 3)
        if band:
            q0, start = window(qb)
            if qb not in hidden:
                rel = (lax.broadcasted_iota(jnp.int32, (band_w, blk), 0)
                       - lax.broadcasted_iota(jnp.int32, (band_w, blk), 1)) + (start - q0)
                hidden[qb] = jnp.where(jnp.abs(rel) <= WINDOW, 0.0, NEG)
        p_ctx, p_band, snk = [], [], []
        for r, head in enumerate(heads):
            cc = slice(r * blk, (r + 1) * blk)
            sink2 = sink_ref[l * N_Q_HEADS + head] * LOG2E
            sc = s_ctx[:, cc]
            m = jnp.maximum(jnp.max(sc, axis=0, keepdims=True), sink2)
            if band:
                sb = s_band[:, cc] + hidden[qb]
                m = jnp.maximum(m, jnp.max(sb, axis=0, keepdims=True))
                p_band.append(jnp.exp2(sb - m).astype(BF16))
            p_ctx.append(jnp.exp2(sc - m).astype(BF16))
            snk.append(jnp.exp2(sink2 - m))
        o_lo = _dot(vclo_ref[0, kv_rows, :], jnp.concatenate(p_ctx[0:2], axis=1))
        o_hi = _dot(vchi_ref[0, kv_rows, :], jnp.concatenate(p_ctx[2:4], axis=1))
        if band:
            o_lo = o_lo + _dot(vlo_ref[0, kv_rows, pl.ds(start, band_w)], jnp.concatenate(p_band[0:2], axis=1))
            o_hi = o_hi + _dot(vhi_ref[0, kv_rows, pl.ds(start, band_w)], jnp.concatenate(p_band[2:4], axis=1))
        num = jnp.where(dims_first, o_lo, o_hi)
        den = jnp.where(dims_first,
                        pltpu.roll(o_lo, HEAD_DIM, 0) + jnp.concatenate(snk[0:2], axis=1),
                        pltpu.roll(o_hi, HEAD_DIM, 0) + jnp.concatenate(snk[2:4], axis=1))
        o_t = num / den
        return [o_t[:, 0:blk].T, o_t[:, blk:2 * blk].T]

    merge_rows = min(tq, MIX_MERGE_ROWS)
    blocks_per_merge = merge_rows // blk
    layer = slice(l, l + 1)
    post_gain = gt_ref[0] * gpost_ref[layer, :]
    ffn_gain = gffn_ref[layer, :] * (1.0 + sc2_ref[0])

    def merge(mi, tiles):
        rows = slice(mi * merge_rows, (mi + 1) * merge_rows)
        a = jnp.concatenate(
            [jnp.concatenate([tl for h in range(N_KV_HEADS) for tl in tiles[(qb, h)]], axis=-1)
             for qb in range(mi * blocks_per_merge, (mi + 1) * blocks_per_merge)], axis=0)
        za = (_rms(a) * ga_ref[layer, :]).astype(BF16)
        y = _dot(za, wout_ref[0, 0:ATTN_W, :]) + _dot(zcs_ref[0, rows, :], wout_ref[0, ATTN_W:, :])
        x_mid = x_ref[0, rows, :] + _rms(y) * post_gain
        xo_ref[0, rows, :] = x_mid
        h2_ref[0, rows, :] = (_rms(x_mid) * ffn_gain + sh2_ref[0]).astype(BF16)

    tiles = {}
    pending = [scores(*g) for g in groups[:MIX_AHEAD]]
    for gi, (qb, h) in enumerate(groups):
        cur = pending.pop(0)
        if gi + MIX_AHEAD < len(groups):
            pending.append(scores(*groups[gi + MIX_AHEAD]))
        tiles[(qb, h)] = attend(qb, h, *cur)
        if h == N_KV_HEADS - 1 and (qb + 1) % blocks_per_merge == 0:
            merge(qb // blocks_per_merge, tiles)


def _mix(sink, l, q, kk, vlo, vhi, kc, vclo, vchi, zcs, x, mod, row, per_batch,
         g_group, g_post, g_ffn, w_out, tq, casts=()):
    b, t, d = x.shape
    band = kk is not None
    ctx_len = kc.shape[1]
    assert t % tq == 0 and tq % min(tq, MIX_MERGE_ROWS) == 0 and MIX_MERGE_ROWS % ATTN_BLOCK == 0
    tile = lambda w: pl.BlockSpec((1, tq, w), lambda bb, i: (bb, i, 0))
    whole = lambda n, w: pl.BlockSpec((1, n, w), lambda bb, i: (bb, 0, 0))
    in_specs = [pl.BlockSpec(memory_space=pltpu.SMEM), tile(ATTN_W)]
    args = [sink, q]
    whole_t = lambda n: pl.BlockSpec((1, KV_DUP_W, n), lambda bb, i: (bb, 0, 0))
    if band:
        in_specs += [whole(t, KV_DUP_W), whole_t(t), whole_t(t)]
        args += [kk, vlo, vhi]
    in_specs += [whole(ctx_len, KV_DUP_W), whole_t(ctx_len), whole_t(ctx_len)]
    in_specs += [tile(zcs.shape[2]), tile(d),
                 _mod_spec(row, 2, d, per_batch), _mod_spec(row, 4, d, per_batch),
                 _mod_spec(row, 3, d, per_batch),
                 _rows_spec(g_group, cols=ATTN_W), _rows_spec(g_post), _rows_spec(g_ffn),
                 _layer_spec(w_out[0], w_out[1])]
    args += [kc, vclo, vchi, zcs, x, mod, mod, mod, g_group, g_post, g_ffn, w_out[0]]
    grid = (b, t // tq)
    c_in, c_args, c_shapes, c_out = _cast_slots(casts, grid)
    kern = functools.partial(_mix_kernel, band=band, t=t, tq=tq, l=l, n_cast=len(casts))
    return pl.pallas_call(
        kern,
        out_shape=[jax.ShapeDtypeStruct((b, t, d), F32), jax.ShapeDtypeStruct((b, t, d), BF16)] + c_shapes,
        grid=grid,
        in_specs=in_specs + c_in,
        out_specs=[tile(d), tile(d)] + c_out,
        compiler_params=_params("arbitrary", "arbitrary") if casts else _params("parallel", "parallel"),
        name="mix",
    )(*args, *c_args)


def _ffn_kernel(*refs, l, tm, d_ff, tf, n_cast):
    hp_ref, h_ref, hn_ref, x_ref, gt_ref, gpost_ref, wup_ref, wdw_ref, wdn_ref = refs[:9]
    cast_in, o_ref, cast_out = refs[9:9 + n_cast], refs[9 + n_cast], refs[10 + n_cast:10 + 2 * n_cast]
    act_ref, = refs[10 + 2 * n_cast:]
    _cast_rows(cast_in, cast_out)
    i = pl.program_id(1)
    rows = tm + 2 * HALO
    zeros = jnp.zeros((HALO, h_ref.shape[2]), BF16)
    hc = jnp.concatenate([jnp.where(i > 0, hp_ref[0], zeros), h_ref[0],
                          jnp.where(i < pl.num_programs(1) - 1, hn_ref[0], zeros)], axis=0)
    mid = slice(HALO, HALO + tm)

    def conv3(u, w):
        return (pltpu.roll(u, 1, 0)[mid] * w[0:1, :] + u[mid] * w[1:2, :]
                + pltpu.roll(u, rows - 1, 0)[mid] * w[2:3, :])

    n_chunks = d_ff // tf

    def up(c):
        gcols = slice(c * tf, (c + 1) * tf)
        vcols = slice(d_ff + c * tf, d_ff + (c + 1) * tf)
        return _dot(hc, wup_ref[0, :, gcols]), _dot(hc, wup_ref[0, :, vcols])

    nxt = up(0)
    for c in range(n_chunks):
        ug, uv = nxt
        if c + 1 < n_chunks:
            nxt = up(c + 1)
        gcols = slice(c * tf, (c + 1) * tf)
        vcols = slice(d_ff + c * tf, d_ff + (c + 1) * tf)
        gate = conv3(ug, wdw_ref[0, :, gcols])
        val = conv3(uv, wdw_ref[0, :, vcols])
        act_ref[:, gcols] = (gate * jax.nn.sigmoid(gate) * val).astype(BF16)
    y = _dot(act_ref[...], wdn_ref[0])
    o_ref[0] = x_ref[0] + (gt_ref[0] * gpost_ref[l:l + 1, :]) * _rms(y)


def _ffn(h2, x, mod, row, per_batch, l, g_post, w_up, w_dw, w_down, tm, casts=()):
    b, t, d = x.shape
    (w_up, w_up_l), (w_down, w_down_l) = w_up, w_down
    d_ff = w_down.shape[1]
    assert t % tm == 0 and tm % HALO == 0 and d_ff % FFN_TF == 0
    prev, nxt = _halo_specs(tm, t, d)
    tile = pl.BlockSpec((1, tm, d), lambda bb, i: (bb, i, 0))
    grid = (b, t // tm)
    c_in, c_args, c_shapes, c_out = _cast_slots(casts, grid)
    kern = functools.partial(_ffn_kernel, l=l, tm=tm, d_ff=d_ff, tf=FFN_TF, n_cast=len(casts))
    out = pl.pallas_call(
        kern,
        out_shape=[jax.ShapeDtypeStruct((b, t, d), F32)] + c_shapes,
        grid=grid,
        in_specs=[prev, tile, nxt, tile, _mod_spec(row, 5, d, per_batch), _rows_spec(g_post),
                  _layer_spec(w_up, w_up_l, single_buffer=True), _layer_spec(w_dw, l),
                  _layer_spec(w_down, w_down_l, single_buffer=True)] + c_in,
        out_specs=[tile] + c_out,
        scratch_shapes=[pltpu.VMEM((tm, d_ff), BF16)],
        compiler_params=_params("arbitrary", "arbitrary") if casts else _params("parallel", "parallel"),
        name="conv_ffn",
    )(h2, h2, h2, x, mod, g_post, w_up, w_dw, w_down, *c_args)
    return out if casts else out[0]


def _rope_tables(n_tokens):
    lane = np.arange(V7X_LANES)
    row_axis = ((lane % HEAD_DIM) // (2 * ROPE_FREQS)) == 0
    sign = np.where((lane % (2 * ROPE_FREQS)) < ROPE_FREQS, -1.0, 1.0).astype(np.float32)
    inv = ROPE_THETA ** (-jnp.arange(ROPE_FREQS, dtype=F32) / ROPE_FREQS)
    tok = jnp.arange(n_tokens)
    pos = jnp.where(jnp.asarray(row_axis)[None, :], (tok // GRID_W)[:, None], (tok % GRID_W)[:, None])
    ang = pos.astype(F32) * jnp.tile(inv, V7X_LANES // ROPE_FREQS)[None, :]
    return jnp.cos(ang), jnp.sin(ang) * sign[None, :]


def _kv_dup_weights(w_in):
    depth, d, _ = w_in.shape
    heads = w_in[:, :, ATTN_W:ATTN_W + 2 * KV_W].astype(BF16).reshape(depth, d, 2 * N_KV_HEADS, 1, HEAD_DIM)
    return jnp.broadcast_to(heads, (depth, d, 2 * N_KV_HEADS, 2, HEAD_DIM)).reshape(depth, d, 2 * KV_DUP_W)


def kernel(x, c, ctx, c_ctx, w_ada, b_ada, g_pre_mix, g_post_mix, g_pre_ffn, g_post_ffn, w_in, sink,
           w_conf_dw, b_conf_dw, conf_ln_g, conf_ln_b, w_sc_dw, g_group, w_out, w_up, w_ffn_dw, w_down):
    batch, seq, d = x.shape
    depth = w_in.shape[0]
    ctx_len = ctx.shape[1]
    assert batch < MOD_ROWS

    cc = jnp.concatenate([c, c_ctx[None, :], jnp.zeros((MOD_ROWS - batch - 1, d), F32)], axis=0)
    mod, w_in_first, w_out_first = _modulation(cc, w_ada, b_ada, ((w_in, 0), (w_out, 0)))
    mod = mod.reshape(depth * MOD_ROWS, 1, N_MOD * d)
    tables = _rope_tables(seq)
    sink_flat = sink.reshape(-1).astype(F32)
    w_kv = _kv_dup_weights(w_in)
    conv_params = (w_conf_dw, b_conf_dw, conf_ln_g, conf_ln_b, w_sc_dw, g_group)

    kv_outs = [(1, 0, KV_DUP_W, "k"), (1, KV_DUP_W, KV_DUP_W, "v_lo"), (1, KV_DUP_W, KV_DUP_W, "v_hi")]
    all_outs = [(0, 0, ATTN_W, "q")] + kv_outs

    w_in_b, w_out_b = (w_in_first, 0), (w_out_first, 0)

    for l in range(depth):
        update_ctx = l < depth - 1
        row = l * MOD_ROWS
        q, kk, vlo, vhi, zcs = _mixer_in(x, mod, row, True, g_pre_mix, w_in_b, (w_kv, l), l, tables, all_outs,
                                         conv_params, IN_TM)
        if update_ctx:
            qc, kc, vclo, vchi, zcs_c = _mixer_in(ctx, mod, row + batch, False, g_pre_mix, w_in_b, (w_kv, l), l,
                                                  None, all_outs, conv_params, ctx_len)
        else:
            kc, vclo, vchi = _mixer_in(ctx, mod, row + batch, False, g_pre_mix, w_in_b, (w_kv, l), l, None,
                                       kv_outs, None, ctx_len)
        x_mid, h2, w_up_l, w_down_l = _mix(sink_flat, l, q, kk, vlo, vhi, kc, vclo, vchi, zcs, x, mod, row, True,
                                           g_group, g_post_mix, g_pre_ffn, w_out_b, MIX_TQ,
                                           casts=((w_up, l), (w_down, l)))
        w_up_b, w_down_b = (w_up_l, 0), (w_down_l, 0)
        if update_ctx:
            ctx_mid, hc2 = _mix(sink_flat, l, qc, None, None, None, kc, vclo, vchi, zcs_c, ctx, mod,
                                row + batch, False, g_group, g_post_mix, g_pre_ffn, w_out_b, ctx_len)
            ctx = _ffn(hc2, ctx_mid, mod, row + batch, False, l, g_post_ffn, w_up_b, w_ffn_dw, w_down_b,
                       ctx_len)
        if l + 1 < depth:
            x, w_in_next, w_out_next = _ffn(h2, x_mid, mod, row, True, l, g_post_ffn, w_up_b, w_ffn_dw, w_down_b,
                                            FFN_TM, casts=((w_in, l + 1), (w_out, l + 1)))
            w_in_b, w_out_b = (w_in_next, 0), (w_out_next, 0)
        else:
            x = _ffn(h2, x_mid, mod, row, True, l, g_post_ffn, w_up_b, w_ffn_dw, w_down_b, FFN_TM)
    return x
```

```python
import functools

import numpy as np
import jax
import jax.numpy as jnp
from jax import lax
from jax.experimental import pallas as pl
from jax.experimental.pallas import tpu as pltpu

F32 = jnp.float32
BF16 = jnp.bfloat16

V7X_LANES = 128
V7X_SUBLANES = 8
V7X_BF16_SUBLANES = 16
V7X_VMEM_LIMIT_BYTES = 56 * 1024 * 1024

HEAD_DIM = 64
N_Q_HEADS = 8
N_KV_HEADS = 2
GRID_W = 64
WINDOW = 128
ATTN_BLOCK = 128
ROPE_FREQS = HEAD_DIM // 4
ROPE_THETA = 10000.0
EPS = 1e-6
NEG = -1e30
LOG2E = 1.4426950408889634

ATTN_W = N_Q_HEADS * HEAD_DIM
KV_W = N_KV_HEADS * HEAD_DIM
KV_DUP_W = 2 * KV_W
N_MOD = 6
MOD_ROWS = 16

IN_TM = 512
MIX_TQ = 512
MIX_MERGE_ROWS = 256
MIX_AHEAD = 2
FFN_TM = 512
FFN_TF = 256
HALO = V7X_BF16_SUBLANES
CONV_RC = 64
PROJ_CHUNK = 2 * V7X_LANES
MOD_TN = 1536


def _rms(x, eps=EPS):
    return x * lax.rsqrt(jnp.mean(x * x, axis=-1, keepdims=True) + eps)


def _dot(a, b):
    return jnp.dot(a, b, preferred_element_type=F32)


def _dot_nt(a, b):
    return lax.dot_general(a, b, (((1,), (1,)), ((), ())), preferred_element_type=F32)


def _params(*sem):
    return pltpu.CompilerParams(dimension_semantics=sem, vmem_limit_bytes=V7X_VMEM_LIMIT_BYTES)


def _layer_spec(arr, l, cols=None, col_block=0, single_buffer=False):
    _, rows, width = arr.shape
    width = cols or width
    mode = dict(pipeline_mode=pl.Buffered(1)) if single_buffer else {}
    return pl.BlockSpec((1, rows, width), lambda *_: (l, 0, col_block), **mode)


def _rows_spec(arr, cols=None, col_block=0):
    depth, width = arr.shape
    return pl.BlockSpec((depth, cols or width), lambda *_: (0, col_block))


def _mod_spec(row, k, d, per_batch):
    if per_batch:
        return pl.BlockSpec((1, 1, d), lambda b, i: (row + b, 0, k))
    return pl.BlockSpec((1, 1, d), lambda b, i: (row, 0, k))


def _cast_slots(weights, grid):
    steps = grid[0] * grid[1]
    in_specs, args, out_shapes, out_specs = [], [], [], []
    for arr, layer in weights:
        _, r, c = arr.shape
        span = 1
        while r % (steps // span) or (r // (steps // span)) % V7X_BF16_SUBLANES:
            span *= 2
            assert span <= steps, (r, steps)
        rows = r // (steps // span)
        in_specs.append(pl.BlockSpec((1, rows, c), lambda bb, i, layer=layer, span=span:
                                     (layer, (bb * grid[1] + i) // span, 0)))
        out_specs.append(pl.BlockSpec((1, rows, c), lambda bb, i, span=span: (0, (bb * grid[1] + i) // span, 0)))
        out_shapes.append(jax.ShapeDtypeStruct((1, r, c), BF16))
        args.append(arr)
    return in_specs, args, out_shapes, out_specs


def _cast_rows(cast_in, cast_out):
    for src, dst in zip(cast_in, cast_out):
        dst[...] = src[...].astype(BF16)


def _halo_specs(tm, t, d):
    per_tile = tm // HALO
    last = t // HALO - 1
    return (pl.BlockSpec((1, HALO, d), lambda bb, i: (bb, jnp.maximum(i * per_tile - 1, 0), 0)),
            pl.BlockSpec((1, HALO, d), lambda bb, i: (bb, jnp.minimum((i + 1) * per_tile, last), 0)))


def _mod_kernel(*refs, n_cast):
    c_ref, w_ref, b_ref = refs[:3]
    cast_in, o_ref, cast_out = refs[3:3 + n_cast], refs[3 + n_cast], refs[4 + n_cast:]
    _cast_rows(cast_in, cast_out)
    c = c_ref[...]
    a = c * jax.nn.sigmoid(c)
    w = w_ref[0]
    a_hi = a.astype(BF16)
    a_lo = (a - a_hi.astype(F32)).astype(BF16)
    w_hi = w.astype(BF16)
    w_lo = (w - w_hi.astype(F32)).astype(BF16)
    hi_lo = _dot(jnp.concatenate([a_hi, a_lo], axis=0), w_hi)
    acc = hi_lo[0:MOD_ROWS] + hi_lo[MOD_ROWS:2 * MOD_ROWS] + _dot(a_hi, w_lo)
    o_ref[0] = acc + b_ref[0]


def _modulation(cc, w_ada, b_ada, casts):
    depth, d, n = w_ada.shape
    grid = (depth, n // MOD_TN)
    c_in, c_args, c_shapes, c_out = _cast_slots(casts, grid)
    return pl.pallas_call(
        functools.partial(_mod_kernel, n_cast=len(casts)),
        out_shape=[jax.ShapeDtypeStruct((depth, MOD_ROWS, n), F32)] + c_shapes,
        grid=grid,
        in_specs=[
            pl.BlockSpec((MOD_ROWS, d), lambda l, j: (0, 0)),
            pl.BlockSpec((1, d, MOD_TN), lambda l, j: (l, 0, j)),
            pl.BlockSpec((1, 1, MOD_TN), lambda l, j: (l, 0, j)),
        ] + c_in,
        out_specs=[pl.BlockSpec((1, MOD_ROWS, MOD_TN), lambda l, j: (l, 0, j))] + c_out,
        compiler_params=_params("arbitrary", "arbitrary"),
        name="modulation",
    )(cc, w_ada, b_ada.reshape(depth, 1, n), *c_args)


def _project(h_ref, rows, w_refs, out_refs, outs, rope, cos, sin):
    lane = lax.broadcasted_iota(jnp.int32, (1, V7X_LANES), 1)
    dims_first = lax.broadcasted_iota(jnp.int32, (V7X_LANES, 1), 0) < HEAD_DIM
    first_half = (lane % (2 * ROPE_FREQS)) < ROPE_FREQS
    chunks = {}
    for o_ref, (wi, col, width, kind) in zip(out_refs, outs):
        for c0 in range(0, width, PROJ_CHUNK):
            chunks.setdefault((wi, col + c0, col + c0 + PROJ_CHUNK), []).append((o_ref, c0, kind))
    prev = None
    for cols, users in list(chunks.items()) + [(None, None)]:
        cur = (_dot(h_ref[rows, :], w_refs[cols[0]][0, :, cols[1]:cols[2]]), users) if users else None
        if prev is not None:
            p, p_users = prev
            transposed = {}
            for o_ref, c0, kind in p_users:
                tiles = []
                for j in range(PROJ_CHUNK // V7X_LANES):
                    t = p[:, j * V7X_LANES:(j + 1) * V7X_LANES]
                    if kind in ("v_lo", "v_hi"):
                        if j not in transposed:
                            transposed[j] = t.T
                        keep = dims_first if kind == "v_lo" else jnp.logical_not(dims_first)
                        t_t = jnp.where(keep, transposed[j], 1.0)
                        o_ref[0, c0 + j * V7X_LANES:c0 + (j + 1) * V7X_LANES, :] = t_t.astype(BF16)
                        continue
                    if rope:
                        partner = jnp.where(first_half,
                                            pltpu.roll(t, V7X_LANES - ROPE_FREQS, 1),
                                            pltpu.roll(t, ROPE_FREQS, 1))
                        t = t * cos + partner * sin
                    if kind == "q":
                        t = t * (HEAD_DIM ** -0.5 * LOG2E)
                    tiles.append(t.astype(BF16))
                if tiles:
                    o_ref[0, :, c0:c0 + PROJ_CHUNK] = jnp.concatenate(tiles, axis=-1)
        prev = cur
        yield None if cur is None else cur[0][0:V7X_SUBLANES, 0:V7X_LANES]


def _mixer_in_kernel(*refs, l, outs, rope, conv, r_col, tm, conf_k, cw):
    it = iter(refs)
    if conv:
        xp_ref, x_ref, xn_ref = next(it), next(it), next(it)
    else:
        x_ref = next(it)
    sh_ref, sc_ref, g_ref, w_ref, wkv_ref = next(it), next(it), next(it), next(it), next(it)
    w_refs = (w_ref, wkv_ref)
    layer = slice(l, l + 1)
    cos = sin = None
    if rope:
        cos, sin = next(it)[...], next(it)[...]
    if conv:
        wcf_ref, bcf_ref, lng_ref, lnb_ref, wsc_ref, gc_ref, gs_ref = (next(it) for _ in range(7))
    out_refs = [next(it) for _ in outs]
    if conv:
        zcs_ref, h_ref, u_ref, v_ref = next(it), next(it), next(it), next(it)
    else:
        h_ref = next(it)

    gain = g_ref[layer, :] * (1.0 + sc_ref[0])
    shift = sh_ref[0]

    def modulate(xv):
        return (_rms(xv) * gain + shift).astype(BF16)

    if not conv:
        h_ref[...] = modulate(x_ref[0])
        for _ in _project(h_ref, slice(0, tm), w_refs, out_refs, outs, rope, cos, sin):
            pass
        return

    i = pl.program_id(1)
    top_ok = (i > 0).astype(F32)
    bot_ok = (i < pl.num_programs(1) - 1).astype(F32)
    mid = slice(HALO, HALO + tm)
    top = slice(0, HALO)
    bot = slice(HALO + tm, HALO + tm + HALO)
    h_ref[mid, :] = modulate(x_ref[0])
    h_ref[top, :] = modulate(xp_ref[0])
    h_ref[bot, :] = modulate(xn_ref[0])

    def rdot(k):
        return _dot(h_ref[...], w_ref[0, :, r_col + k * cw:r_col + (k + 1) * cw])

    def store_padded(ref, val):
        ref[mid, :] = val[mid]
        ref[top, :] = val[top] * top_ok
        ref[bot, :] = val[bot] * bot_ok

    cv, cg = rdot(0), rdot(1)
    store_padded(u_ref, cv * jax.nn.sigmoid(cg))

    rc = min(CONV_RC, tm)
    win = rc + 2 * HALO
    pad_c = conf_k // 2

    def conformer_chunk(c, after):
        r0 = c * rc
        zero = None
        if after is not None:
            half = jnp.uint32(16)
            bits = lax.shift_right_logical(lax.shift_right_logical(pltpu.bitcast(after, jnp.uint32), half), half)
            zero = pltpu.bitcast(bits, F32)[0:1, 0:1]
        halves = []
        for hh in range(cw // V7X_LANES):
            lanes = slice(hh * V7X_LANES, (hh + 1) * V7X_LANES)
            wdw = wcf_ref[0, :, lanes]
            window = u_ref[r0:r0 + win, lanes]
            acc = None
            for res in range(V7X_SUBLANES):
                shifted = window if res == 0 else pltpu.roll(window, win - res, 0)
                for k in range(conf_k):
                    off = HALO - pad_c + k
                    if off % V7X_SUBLANES != res:
                        continue
                    term = shifted[off - res:off - res + rc, :] * wdw[k:k + 1, :]
                    acc = term if acc is None else acc + term
            halves.append(acc)
        bias = bcf_ref[layer, :] if zero is None else bcf_ref[layer, :] + zero
        y = jnp.concatenate(halves, axis=-1) + bias
        cen = y - jnp.mean(y, axis=-1, keepdims=True)
        var = jnp.mean(cen * cen, axis=-1, keepdims=True)
        ln = cen * lax.rsqrt(var + EPS) * lng_ref[layer, :] + lnb_ref[layer, :]
        cf = ln * jax.nn.sigmoid(ln)
        zcs_ref[0, r0:r0 + rc, 0:cw] = (_rms(cf) * gc_ref[layer, :]).astype(BF16)

    def short_conv(sb):
        vwin = v_ref[...]
        wsc = wsc_ref[0]
        s = sb[mid] * (pltpu.roll(vwin, 1, 0)[mid] * wsc[0:1, :] + vwin[mid] * wsc[1:2, :]
                       + pltpu.roll(vwin, tm + 2 * HALO - 1, 0)[mid] * wsc[2:3, :])
        zcs_ref[0, :, cw:2 * cw] = (_rms(s) * gs_ref[layer, :]).astype(BF16)

    piece = lambda val: val[0:V7X_SUBLANES, 0:V7X_LANES]
    short_in = {}

    def mxu_work():
        short_in["sb"] = rdot(2)
        yield piece(short_in["sb"])
        scg = rdot(3)
        yield piece(scg)
        su = rdot(4)
        store_padded(v_ref, scg * su)
        yield piece(su)
        yield from _project(h_ref, mid, w_refs, out_refs, outs, rope, cos, sin)

    work = mxu_work()
    pending = None
    for c in range(tm // rc):
        after, pending = pending, next(work, None)
        conformer_chunk(c, after)
    for _ in work:
        pass
    short_conv(short_in["sb"])


def _mixer_in(x, mod, row, per_batch, g_pre, w, w_kv, l, tables, outs, conv_params, tm):
    b, t, d = x.shape
    (w, w_l), (w_kv, w_kv_l) = w, w_kv
    n = w.shape[2]
    rope = tables is not None
    conv = conv_params is not None
    widths = [o[2] for o in outs]
    assert t % tm == 0 and tm % HALO == 0 and all(o[1] + o[2] <= (w, w_kv)[o[0]].shape[2] for o in outs)
    tile = lambda wd: pl.BlockSpec((1, tm, wd), lambda bb, i: (bb, i, 0))
    in_specs, args = [], []
    if conv:
        prev, nxt = _halo_specs(tm, t, d)
        in_specs += [prev, tile(d), nxt]
        args += [x, x, x]
    else:
        in_specs += [tile(d)]
        args += [x]
    in_specs += [_mod_spec(row, 0, d, per_batch), _mod_spec(row, 1, d, per_batch),
                 _rows_spec(g_pre), _layer_spec(w, w_l, single_buffer=True),
                 _layer_spec(w_kv, w_kv_l, single_buffer=True)]
    args += [mod, mod, g_pre, w, w_kv]
    if rope:
        in_specs += [pl.BlockSpec((tm, V7X_LANES), lambda bb, i: (i, 0))] * 2
        args += list(tables)
    is_t = [o[3] in ("v_lo", "v_hi") for o in outs]
    out_shape = [jax.ShapeDtypeStruct((b, wd, t) if tr else (b, t, wd), BF16) for wd, tr in zip(widths, is_t)]
    out_specs = [pl.BlockSpec((1, wd, tm), lambda bb, i: (bb, 0, i)) if tr else tile(wd)
                 for wd, tr in zip(widths, is_t)]
    scratch = [pltpu.VMEM((tm, d), BF16)]
    conf_k = cw = r_col = 0
    if conv:
        w_conf_dw, b_conf_dw, ln_g, ln_b, w_sc_dw, g_group = conv_params
        conf_k, cw = w_conf_dw.shape[1:]
        r_col = n - 5 * cw
        assert w_sc_dw.shape[2] == cw and conf_k // 2 < HALO and g_group.shape[1] == ATTN_W + 2 * cw
        in_specs += [_layer_spec(w_conf_dw, l), _rows_spec(b_conf_dw), _rows_spec(ln_g), _rows_spec(ln_b),
                     _layer_spec(w_sc_dw, l),
                     _rows_spec(g_group, cols=cw, col_block=ATTN_W // cw),
                     _rows_spec(g_group, cols=cw, col_block=ATTN_W // cw + 1)]
        args += [w_conf_dw, b_conf_dw, ln_g, ln_b, w_sc_dw, g_group, g_group]
        out_shape.append(jax.ShapeDtypeStruct((b, t, 2 * cw), BF16))
        out_specs.append(tile(2 * cw))
        rows = tm + 2 * HALO
        scratch = [pltpu.VMEM((rows, d), BF16), pltpu.VMEM((rows, cw), F32), pltpu.VMEM((rows, cw), F32)]
    kern = functools.partial(_mixer_in_kernel, l=l, outs=tuple(outs), rope=rope, conv=conv, r_col=r_col,
                             tm=tm, conf_k=conf_k, cw=cw)
    return pl.pallas_call(
        kern,
        out_shape=out_shape,
        grid=(b, t // tm),
        in_specs=in_specs,
        out_specs=out_specs,
        scratch_shapes=scratch,
        compiler_params=_params("parallel", "parallel"),
        name="mixer_in",
    )(*args)


def _mix_kernel(*refs, band, t, tq, l, n_cast):
    refs = list(refs)
    if n_cast:
        cast_out = refs[-n_cast:]
        cast_in = refs[-2 - 2 * n_cast:-2 - n_cast]
        refs = refs[:-2 - 2 * n_cast] + refs[-2 - n_cast:-n_cast]
        _cast_rows(cast_in, cast_out)
    if band:
        (sink_ref, q_ref, kk_ref, vlo_ref, vhi_ref, kc_ref, vclo_ref, vchi_ref, zcs_ref, x_ref,
         gt_ref, sc2_ref, sh2_ref, ga_ref, gpost_ref, gffn_ref, wout_ref, xo_ref, h2_ref) = refs
    else:
        (sink_ref, q_ref, kc_ref, vclo_ref, vchi_ref, zcs_ref, x_ref,
         gt_ref, sc2_ref, sh2_ref, ga_ref, gpost_ref, gffn_ref, wout_ref, xo_ref, h2_ref) = refs
    blk = ATTN_BLOCK
    band_w = blk + 2 * WINDOW
    lane = lax.broadcasted_iota(jnp.int32, (1, V7X_LANES), 1)
    lo_lanes = lane < HEAD_DIM
    m_lo = jnp.where(lo_lanes, 1.0, 0.0).astype(BF16)
    m_hi = jnp.where(lo_lanes, 0.0, 1.0).astype(BF16)
    t0 = pl.program_id(1) * tq
    n_blocks = tq // blk
    groups = [(qb, h) for qb in range(n_blocks) for h in range(N_KV_HEADS)]

    def window(qb):
        q0 = t0 + qb * blk
        return q0, pl.multiple_of(jnp.clip(q0 - WINDOW, 0, t - band_w), blk)

    def scores(qb, h):
        rows = slice(qb * blk, (qb + 1) * blk)
        kv_lanes = slice(h * V7X_LANES, (h + 1) * V7X_LANES)
        qp0 = q_ref[0, rows, (2 * h) * V7X_LANES:(2 * h + 1) * V7X_LANES]
        qp1 = q_ref[0, rows, (2 * h + 1) * V7X_LANES:(2 * h + 2) * V7X_LANES]
        qs = jnp.concatenate([qp0 * m_lo, qp1 * m_lo, qp0 * m_hi, qp1 * m_hi], axis=0)
        s_ctx = _dot_nt(kc_ref[0, :, kv_lanes], qs)
        s_band = None
        if band:
            _, start = window(qb)
            s_band = _dot_nt(kk_ref[0, pl.ds(start, band_w), kv_lanes], qs)
        return s_ctx, s_band

    hidden = {}
    dims_first = lax.broadcasted_iota(jnp.int32, (V7X_LANES, 1), 0) < HEAD_DIM

    def attend(qb, h, s_ctx, s_band):
        kv_rows = slice(h * V7X_LANES, (h + 1) * V7X_LANES)
        heads = (4 * h, 4 * h + 2, 4 * h + 1, 4 * h + 3)
        if band:
            q0, start = window(qb)
            if qb not in hidden:
                rel = (lax.broadcasted_iota(jnp.int32, (band_w, blk), 0)
                       - lax.broadcasted_iota(jnp.int32, (band_w, blk), 1)) + (start - q0)
                hidden[qb] = jnp.where(jnp.abs(rel) <= WINDOW, 0.0, NEG)
        p_ctx, p_band, snk = [], [], []
        for r, head in enumerate(heads):
            cc = slice(r * blk, (r + 1) * blk)
            sink2 = sink_ref[l * N_Q_HEADS + head] * LOG2E
            sc = s_ctx[:, cc]
            m = jnp.maximum(jnp.max(sc, axis=0, keepdims=True), sink2)
            if band:
                sb = s_band[:, cc] + hidden[qb]
                m = jnp.maximum(m, jnp.max(sb, axis=0, keepdims=True))
                p_band.append(jnp.exp2((sb - m).astype(BF16)))
            p_ctx.append(jnp.exp2((sc - m).astype(BF16)))
            snk.append(jnp.exp2(sink2 - m))
        o_lo = _dot(vclo_ref[0, kv_rows, :], jnp.concatenate(p_ctx[0:2], axis=1))
        o_hi = _dot(vchi_ref[0, kv_rows, :], jnp.concatenate(p_ctx[2:4], axis=1))
        if band:
            o_lo = o_lo + _dot(vlo_ref[0, kv_rows, pl.ds(start, band_w)], jnp.concatenate(p_band[0:2], axis=1))
            o_hi = o_hi + _dot(vhi_ref[0, kv_rows, pl.ds(start, band_w)], jnp.concatenate(p_band[2:4], axis=1))
        num = jnp.where(dims_first, o_lo, o_hi)
        den = jnp.where(dims_first,
                        pltpu.roll(o_lo, HEAD_DIM, 0) + jnp.concatenate(snk[0:2], axis=1),
                        pltpu.roll(o_hi, HEAD_DIM, 0) + jnp.concatenate(snk[2:4], axis=1))
        o_t = num / den
        return [o_t[:, 0:blk].T, o_t[:, blk:2 * blk].T]

    merge_rows = min(tq, MIX_MERGE_ROWS)
    blocks_per_merge = merge_rows // blk
    layer = slice(l, l + 1)
    post_gain = gt_ref[0] * gpost_ref[layer, :]
    ffn_gain = gffn_ref[layer, :] * (1.0 + sc2_ref[0])

    def merge(mi, tiles):
        rows = slice(mi * merge_rows, (mi + 1) * merge_rows)
        a = jnp.concatenate(
            [jnp.concatenate([tl for h in range(N_KV_HEADS) for tl in tiles[(qb, h)]], axis=-1)
             for qb in range(mi * blocks_per_merge, (mi + 1) * blocks_per_merge)], axis=0)
        za = (_rms(a) * ga_ref[layer, :]).astype(BF16)
        y = _dot(za, wout_ref[0, 0:ATTN_W, :]) + _dot(zcs_ref[0, rows, :], wout_ref[0, ATTN_W:, :])
        x_mid = x_ref[0, rows, :] + _rms(y) * post_gain
        xo_ref[0, rows, :] = x_mid
        h2_ref[0, rows, :] = (_rms(x_mid) * ffn_gain + sh2_ref[0]).astype(BF16)

    tiles = {}
    pending = [scores(*g) for g in groups[:MIX_AHEAD]]
    for gi, (qb, h) in enumerate(groups):
        cur = pending.pop(0)
        if gi + MIX_AHEAD < len(groups):
            pending.append(scores(*groups[gi + MIX_AHEAD]))
        tiles[(qb, h)] = attend(qb, h, *cur)
        if h == N_KV_HEADS - 1 and (qb + 1) % blocks_per_merge == 0:
            merge(qb // blocks_per_merge, tiles)


def _mix(sink, l, q, kk, vlo, vhi, kc, vclo, vchi, zcs, x, mod, row, per_batch,
         g_group, g_post, g_ffn, w_out, tq, casts=()):
    b, t, d = x.shape
    band = kk is not None
    ctx_len = kc.shape[1]
    assert t % tq == 0 and tq % min(tq, MIX_MERGE_ROWS) == 0 and MIX_MERGE_ROWS % ATTN_BLOCK == 0
    tile = lambda w: pl.BlockSpec((1, tq, w), lambda bb, i: (bb, i, 0))
    whole = lambda n, w: pl.BlockSpec((1, n, w), lambda bb, i: (bb, 0, 0))
    in_specs = [pl.BlockSpec(memory_space=pltpu.SMEM), tile(ATTN_W)]
    args = [sink, q]
    whole_t = lambda n: pl.BlockSpec((1, KV_DUP_W, n), lambda bb, i: (bb, 0, 0))
    if band:
        in_specs += [whole(t, KV_DUP_W), whole_t(t), whole_t(t)]
        args += [kk, vlo, vhi]
    in_specs += [whole(ctx_len, KV_DUP_W), whole_t(ctx_len), whole_t(ctx_len)]
    in_specs += [tile(zcs.shape[2]), tile(d),
                 _mod_spec(row, 2, d, per_batch), _mod_spec(row, 4, d, per_batch),
                 _mod_spec(row, 3, d, per_batch),
                 _rows_spec(g_group, cols=ATTN_W), _rows_spec(g_post), _rows_spec(g_ffn),
                 _layer_spec(w_out[0], w_out[1])]
    args += [kc, vclo, vchi, zcs, x, mod, mod, mod, g_group, g_post, g_ffn, w_out[0]]
    grid = (b, t // tq)
    c_in, c_args, c_shapes, c_out = _cast_slots(casts, grid)
    kern = functools.partial(_mix_kernel, band=band, t=t, tq=tq, l=l, n_cast=len(casts))
    return pl.pallas_call(
        kern,
        out_shape=[jax.ShapeDtypeStruct((b, t, d), F32), jax.ShapeDtypeStruct((b, t, d), BF16)] + c_shapes,
        grid=grid,
        in_specs=in_specs + c_in,
        out_specs=[tile(d), tile(d)] + c_out,
        compiler_params=_params("arbitrary", "arbitrary") if casts else _params("parallel", "parallel"),
        name="mix",
    )(*args, *c_args)


def _ffn_kernel(*refs, l, tm, d_ff, tf, n_cast):
    hp_ref, h_ref, hn_ref, x_ref, gt_ref, gpost_ref, wup_ref, wdw_ref, wdn_ref = refs[:9]
    cast_in, o_ref, cast_out = refs[9:9 + n_cast], refs[9 + n_cast], refs[10 + n_cast:10 + 2 * n_cast]
    act_ref, = refs[10 + 2 * n_cast:]
    _cast_rows(cast_in, cast_out)
    i = pl.program_id(1)
    rows = tm + 2 * HALO
    zeros = jnp.zeros((HALO, h_ref.shape[2]), BF16)
    hc = jnp.concatenate([jnp.where(i > 0, hp_ref[0], zeros), h_ref[0],
                          jnp.where(i < pl.num_programs(1) - 1, hn_ref[0], zeros)], axis=0)
    mid = slice(HALO, HALO + tm)

    def conv3(u, w):
        return (pltpu.roll(u, 1, 0)[mid] * w[0:1, :] + u[mid] * w[1:2, :]
                + pltpu.roll(u, rows - 1, 0)[mid] * w[2:3, :])

    n_chunks = d_ff // tf

    def up(c):
        gcols = slice(c * tf, (c + 1) * tf)
        vcols = slice(d_ff + c * tf, d_ff + (c + 1) * tf)
        return _dot(hc, wup_ref[0, :, gcols]), _dot(hc, wup_ref[0, :, vcols])

    nxt = up(0)
    for c in range(n_chunks):
        ug, uv = nxt
        if c + 1 < n_chunks:
            nxt = up(c + 1)
        gcols = slice(c * tf, (c + 1) * tf)
        vcols = slice(d_ff + c * tf, d_ff + (c + 1) * tf)
        gate = conv3(ug, wdw_ref[0, :, gcols])
        val = conv3(uv, wdw_ref[0, :, vcols])
        act_ref[:, gcols] = (gate * jax.nn.sigmoid(gate) * val).astype(BF16)
    y = _dot(act_ref[...], wdn_ref[0])
    o_ref[0] = x_ref[0] + (gt_ref[0] * gpost_ref[l:l + 1, :]) * _rms(y)


def _ffn(h2, x, mod, row, per_batch, l, g_post, w_up, w_dw, w_down, tm, casts=()):
    b, t, d = x.shape
    (w_up, w_up_l), (w_down, w_down_l) = w_up, w_down
    d_ff = w_down.shape[1]
    assert t % tm == 0 and tm % HALO == 0 and d_ff % FFN_TF == 0
    prev, nxt = _halo_specs(tm, t, d)
    tile = pl.BlockSpec((1, tm, d), lambda bb, i: (bb, i, 0))
    grid = (b, t // tm)
    c_in, c_args, c_shapes, c_out = _cast_slots(casts, grid)
    kern = functools.partial(_ffn_kernel, l=l, tm=tm, d_ff=d_ff, tf=FFN_TF, n_cast=len(casts))
    out = pl.pallas_call(
        kern,
        out_shape=[jax.ShapeDtypeStruct((b, t, d), F32)] + c_shapes,
        grid=grid,
        in_specs=[prev, tile, nxt, tile, _mod_spec(row, 5, d, per_batch), _rows_spec(g_post),
                  _layer_spec(w_up, w_up_l, single_buffer=True), _layer_spec(w_dw, l),
                  _layer_spec(w_down, w_down_l, single_buffer=True)] + c_in,
        out_specs=[tile] + c_out,
        scratch_shapes=[pltpu.VMEM((tm, d_ff), BF16)],
        compiler_params=_params("arbitrary", "arbitrary") if casts else _params("parallel", "parallel"),
        name="conv_ffn",
    )(h2, h2, h2, x, mod, g_post, w_up, w_dw, w_down, *c_args)
    return out if casts else out[0]


def _rope_tables(n_tokens):
    lane = np.arange(V7X_LANES)
    row_axis = ((lane % HEAD_DIM) // (2 * ROPE_FREQS)) == 0
    sign = np.where((lane % (2 * ROPE_FREQS)) < ROPE_FREQS, -1.0, 1.0).astype(np.float32)
    inv = ROPE_THETA ** (-jnp.arange(ROPE_FREQS, dtype=F32) / ROPE_FREQS)
    tok = jnp.arange(n_tokens)
    pos = jnp.where(jnp.asarray(row_axis)[None, :], (tok // GRID_W)[:, None], (tok % GRID_W)[:, None])
    ang = pos.astype(F32) * jnp.tile(inv, V7X_LANES // ROPE_FREQS)[None, :]
    return jnp.cos(ang), jnp.sin(ang) * sign[None, :]


def _kv_dup_weights(w_in):
    depth, d, _ = w_in.shape
    heads = w_in[:, :, ATTN_W:ATTN_W + 2 * KV_W].astype(BF16).reshape(depth, d, 2 * N_KV_HEADS, 1, HEAD_DIM)
    return jnp.broadcast_to(heads, (depth, d, 2 * N_KV_HEADS, 2, HEAD_DIM)).reshape(depth, d, 2 * KV_DUP_W)


def kernel(x, c, ctx, c_ctx, w_ada, b_ada, g_pre_mix, g_post_mix, g_pre_ffn, g_post_ffn, w_in, sink,
           w_conf_dw, b_conf_dw, conf_ln_g, conf_ln_b, w_sc_dw, g_group, w_out, w_up, w_ffn_dw, w_down):
    batch, seq, d = x.shape
    depth = w_in.shape[0]
    ctx_len = ctx.shape[1]
    assert batch < MOD_ROWS

    cc = jnp.concatenate([c, c_ctx[None, :], jnp.zeros((MOD_ROWS - batch - 1, d), F32)], axis=0)
    mod, w_in_first, w_out_first = _modulation(cc, w_ada, b_ada, ((w_in, 0), (w_out, 0)))
    mod = mod.reshape(depth * MOD_ROWS, 1, N_MOD * d)
    tables = _rope_tables(seq)
    sink_flat = sink.reshape(-1).astype(F32)
    w_kv = _kv_dup_weights(w_in)
    conv_params = (w_conf_dw, b_conf_dw, conf_ln_g, conf_ln_b, w_sc_dw, g_group)

    kv_outs = [(1, 0, KV_DUP_W, "k"), (1, KV_DUP_W, KV_DUP_W, "v_lo"), (1, KV_DUP_W, KV_DUP_W, "v_hi")]
    all_outs = [(0, 0, ATTN_W, "q")] + kv_outs

    w_in_b, w_out_b = (w_in_first, 0), (w_out_first, 0)

    for l in range(depth):
        update_ctx = l < depth - 1
        row = l * MOD_ROWS
        q, kk, vlo, vhi, zcs = _mixer_in(x, mod, row, True, g_pre_mix, w_in_b, (w_kv, l), l, tables, all_outs,
                                         conv_params, IN_TM)
        if update_ctx:
            qc, kc, vclo, vchi, zcs_c = _mixer_in(ctx, mod, row + batch, False, g_pre_mix, w_in_b, (w_kv, l), l,
                                                  None, all_outs, conv_params, ctx_len)
        else:
            kc, vclo, vchi = _mixer_in(ctx, mod, row + batch, False, g_pre_mix, w_in_b, (w_kv, l), l, None,
                                       kv_outs, None, ctx_len)
        x_mid, h2, w_up_l, w_down_l = _mix(sink_flat, l, q, kk, vlo, vhi, kc, vclo, vchi, zcs, x, mod, row, True,
                                           g_group, g_post_mix, g_pre_ffn, w_out_b, MIX_TQ,
                                           casts=((w_up, l), (w_down, l)))
        w_up_b, w_down_b = (w_up_l, 0), (w_down_l, 0)
        if update_ctx:
            ctx_mid, hc2 = _mix(sink_flat, l, qc, None, None, None, kc, vclo, vchi, zcs_c, ctx, mod,
                                row + batch, False, g_group, g_post_mix, g_pre_ffn, w_out_b, ctx_len)
            ctx = _ffn(hc2, ctx_mid, mod, row + batch, False, l, g_post_ffn, w_up_b, w_ffn_dw, w_down_b,
                       ctx_len)
        if l + 1 < depth:
            x, w_in_next, w_out_next = _ffn(h2, x_mid, mod, row, True, l, g_post_ffn, w_up_b, w_ffn_dw, w_down_b,
                                            FFN_TM, casts=((w_in, l + 1), (w_out, l + 1)))
            w_in_b, w_out_b = (w_in_next, 0), (w_out_next, 0)
        else:
            x = _ffn(h2, x_mid, mod, row, True, l, g_post_ffn, w_up_b, w_ffn_dw, w_down_b, FFN_TM)
    return x
```

```python
import functools

import numpy as np
import jax
import jax.numpy as jnp
from jax import lax
from jax.experimental import pallas as pl
from jax.experimental.pallas import tpu as pltpu

F32 = jnp.float32
BF16 = jnp.bfloat16

V7X_LANES = 128
V7X_SUBLANES = 8
V7X_BF16_SUBLANES = 16
V7X_VMEM_LIMIT_BYTES = 56 * 1024 * 1024

HEAD_DIM = 64
N_Q_HEADS = 8
N_KV_HEADS = 2
GRID_W = 64
WINDOW = 128
ATTN_BLOCK = 128
ROPE_FREQS = HEAD_DIM // 4
ROPE_THETA = 10000.0
EPS = 1e-6
NEG = -1e30
LOG2E = 1.4426950408889634

ATTN_W = N_Q_HEADS * HEAD_DIM
KV_W = N_KV_HEADS * HEAD_DIM
KV_DUP_W = 2 * KV_W
N_MOD = 6
MOD_ROWS = 16

IN_TM = 512
MIX_TQ = 512
MIX_MERGE_ROWS = 256
MIX_AHEAD = 2
FFN_TM = 512
FFN_TF = 256
HALO = V7X_BF16_SUBLANES
CONV_RC = 64
PROJ_CHUNK = 2 * V7X_LANES
MOD_TN = 1536


def _rms(x, eps=EPS):
    return x * lax.rsqrt(jnp.mean(x * x, axis=-1, keepdims=True) + eps)


def _dot(a, b):
    return jnp.dot(a, b, preferred_element_type=F32)


def _dot_nt(a, b):
    return lax.dot_general(a, b, (((1,), (1,)), ((), ())), preferred_element_type=F32)


def _params(*sem):
    return pltpu.CompilerParams(dimension_semantics=sem, vmem_limit_bytes=V7X_VMEM_LIMIT_BYTES)


def _layer_spec(arr, l, cols=None, col_block=0, single_buffer=False):
    _, rows, width = arr.shape
    width = cols or width
    mode = dict(pipeline_mode=pl.Buffered(1)) if single_buffer else {}
    return pl.BlockSpec((1, rows, width), lambda *_: (l, 0, col_block), **mode)


def _rows_spec(arr, cols=None, col_block=0):
    depth, width = arr.shape
    return pl.BlockSpec((depth, cols or width), lambda *_: (0, col_block))


def _mod_spec(row, k, d, per_batch):
    if per_batch:
        return pl.BlockSpec((1, 1, d), lambda b, i: (row + b, 0, k))
    return pl.BlockSpec((1, 1, d), lambda b, i: (row, 0, k))


def _cast_slots(weights, grid):
    steps = grid[0] * grid[1]
    in_specs, args, out_shapes, out_specs = [], [], [], []
    for arr, layer in weights:
        _, r, c = arr.shape
        span = 1
        while r % (steps // span) or (r // (steps // span)) % V7X_BF16_SUBLANES:
            span *= 2
            assert span <= steps, (r, steps)
        rows = r // (steps // span)
        in_specs.append(pl.BlockSpec((1, rows, c), lambda bb, i, layer=layer, span=span:
                                     (layer, (bb * grid[1] + i) // span, 0)))
        out_specs.append(pl.BlockSpec((1, rows, c), lambda bb, i, span=span: (0, (bb * grid[1] + i) // span, 0)))
        out_shapes.append(jax.ShapeDtypeStruct((1, r, c), BF16))
        args.append(arr)
    return in_specs, args, out_shapes, out_specs


def _cast_rows(cast_in, cast_out):
    for src, dst in zip(cast_in, cast_out):
        dst[...] = src[...].astype(BF16)


def _halo_specs(tm, t, d):
    per_tile = tm // HALO
    last = t // HALO - 1
    return (pl.BlockSpec((1, HALO, d), lambda bb, i: (bb, jnp.maximum(i * per_tile - 1, 0), 0)),
            pl.BlockSpec((1, HALO, d), lambda bb, i: (bb, jnp.minimum((i + 1) * per_tile, last), 0)))


def _mod_kernel(*refs, n_cast):
    c_ref, w_ref, b_ref = refs[:3]
    cast_in, o_ref, cast_out = refs[3:3 + n_cast], refs[3 + n_cast], refs[4 + n_cast:]
    _cast_rows(cast_in, cast_out)
    c = c_ref[...]
    a = c * jax.nn.sigmoid(c)
    w = w_ref[0]
    a_hi = a.astype(BF16)
    a_lo = (a - a_hi.astype(F32)).astype(BF16)
    w_hi = w.astype(BF16)
    w_lo = (w - w_hi.astype(F32)).astype(BF16)
    hi_lo = _dot(jnp.concatenate([a_hi, a_lo], axis=0), w_hi)
    acc = hi_lo[0:MOD_ROWS] + hi_lo[MOD_ROWS:2 * MOD_ROWS] + _dot(a_hi, w_lo)
    o_ref[0] = acc + b_ref[0]


def _modulation(cc, w_ada, b_ada, casts):
    depth, d, n = w_ada.shape
    grid = (depth, n // MOD_TN)
    c_in, c_args, c_shapes, c_out = _cast_slots(casts, grid)
    return pl.pallas_call(
        functools.partial(_mod_kernel, n_cast=len(casts)),
        out_shape=[jax.ShapeDtypeStruct((depth, MOD_ROWS, n), F32)] + c_shapes,
        grid=grid,
        in_specs=[
            pl.BlockSpec((MOD_ROWS, d), lambda l, j: (0, 0)),
            pl.BlockSpec((1, d, MOD_TN), lambda l, j: (l, 0, j)),
            pl.BlockSpec((1, 1, MOD_TN), lambda l, j: (l, 0, j)),
        ] + c_in,
        out_specs=[pl.BlockSpec((1, MOD_ROWS, MOD_TN), lambda l, j: (l, 0, j))] + c_out,
        compiler_params=_params("arbitrary", "arbitrary"),
        name="modulation",
    )(cc, w_ada, b_ada.reshape(depth, 1, n), *c_args)


def _project(h_ref, rows, w_refs, out_refs, outs, rope, cos, sin):
    lane = lax.broadcasted_iota(jnp.int32, (1, V7X_LANES), 1)
    dims_first = lax.broadcasted_iota(jnp.int32, (V7X_LANES, 1), 0) < HEAD_DIM
    first_half = (lane % (2 * ROPE_FREQS)) < ROPE_FREQS
    chunks = {}
    for o_ref, (wi, col, width, kind) in zip(out_refs, outs):
        for c0 in range(0, width, PROJ_CHUNK):
            chunks.setdefault((wi, col + c0, col + c0 + PROJ_CHUNK), []).append((o_ref, c0, kind))
    prev = None
    for cols, users in list(chunks.items()) + [(None, None)]:
        cur = (_dot(h_ref[rows, :], w_refs[cols[0]][0, :, cols[1]:cols[2]]), users) if users else None
        if prev is not None:
            p, p_users = prev
            transposed = {}
            for o_ref, c0, kind in p_users:
                tiles = []
                for j in range(PROJ_CHUNK // V7X_LANES):
                    t = p[:, j * V7X_LANES:(j + 1) * V7X_LANES]
                    if kind in ("v_lo", "v_hi"):
                        if j not in transposed:
                            transposed[j] = t.T
                        keep = dims_first if kind == "v_lo" else jnp.logical_not(dims_first)
                        t_t = jnp.where(keep, transposed[j], 1.0)
                        o_ref[0, c0 + j * V7X_LANES:c0 + (j + 1) * V7X_LANES, :] = t_t.astype(BF16)
                        continue
                    if rope:
                        partner = jnp.where(first_half,
                                            pltpu.roll(t, V7X_LANES - ROPE_FREQS, 1),
                                            pltpu.roll(t, ROPE_FREQS, 1))
                        t = t * cos + partner * sin
                    if kind == "q":
                        t = t * (HEAD_DIM ** -0.5 * LOG2E)
                    tiles.append(t.astype(BF16))
                if tiles:
                    o_ref[0, :, c0:c0 + PROJ_CHUNK] = jnp.concatenate(tiles, axis=-1)
        prev = cur
        yield None if cur is None else cur[0][0:V7X_SUBLANES, 0:V7X_LANES]


def _mixer_in_kernel(*refs, l, outs, rope, conv, r_col, tm, conf_k, cw):
    it = iter(refs)
    if conv:
        xp_ref, x_ref, xn_ref = next(it), next(it), next(it)
    else:
        x_ref = next(it)
    sh_ref, sc_ref, g_ref, w_ref, wkv_ref = next(it), next(it), next(it), next(it), next(it)
    w_refs = (w_ref, wkv_ref)
    layer = slice(l, l + 1)
    cos = sin = None
    if rope:
        cos, sin = next(it)[...], next(it)[...]
    if conv:
        wcf_ref, bcf_ref, lng_ref, lnb_ref, wsc_ref, gc_ref, gs_ref = (next(it) for _ in range(7))
    out_refs = [next(it) for _ in outs]
    if conv:
        zcs_ref, h_ref, u_ref, v_ref = next(it), next(it), next(it), next(it)
    else:
        h_ref = next(it)

    gain = g_ref[layer, :] * (1.0 + sc_ref[0])
    shift = sh_ref[0]

    def modulate(xv):
        return (_rms(xv) * gain + shift).astype(BF16)

    if not conv:
        h_ref[...] = modulate(x_ref[0])
        for _ in _project(h_ref, slice(0, tm), w_refs, out_refs, outs, rope, cos, sin):
            pass
        return

    i = pl.program_id(1)
    top_ok = (i > 0).astype(F32)
    bot_ok = (i < pl.num_programs(1) - 1).astype(F32)
    mid = slice(HALO, HALO + tm)
    top = slice(0, HALO)
    bot = slice(HALO + tm, HALO + tm + HALO)
    h_ref[mid, :] = modulate(x_ref[0])
    h_ref[top, :] = modulate(xp_ref[0])
    h_ref[bot, :] = modulate(xn_ref[0])

    def rdot(k):
        return _dot(h_ref[...], w_ref[0, :, r_col + k * cw:r_col + (k + 1) * cw])

    def store_padded(ref, val):
        ref[mid, :] = val[mid]
        ref[top, :] = val[top] * top_ok
        ref[bot, :] = val[bot] * bot_ok

    cv, cg = rdot(0), rdot(1)
    store_padded(u_ref, cv * jax.nn.sigmoid(cg))

    rc = min(CONV_RC, tm)
    win = rc + 2 * HALO
    pad_c = conf_k // 2

    def conformer_chunk(c, after):
        r0 = c * rc
        zero = None
        if after is not None:
            half = jnp.uint32(16)
            bits = lax.shift_right_logical(lax.shift_right_logical(pltpu.bitcast(after, jnp.uint32), half), half)
            zero = pltpu.bitcast(bits, F32)[0:1, 0:1]
        halves = []
        for hh in range(cw // V7X_LANES):
            lanes = slice(hh * V7X_LANES, (hh + 1) * V7X_LANES)
            wdw = wcf_ref[0, :, lanes]
            window = u_ref[r0:r0 + win, lanes]
            acc = None
            for res in range(V7X_SUBLANES):
                shifted = window if res == 0 else pltpu.roll(window, win - res, 0)
                for k in range(conf_k):
                    off = HALO - pad_c + k
                    if off % V7X_SUBLANES != res:
                        continue
                    term = shifted[off - res:off - res + rc, :] * wdw[k:k + 1, :]
                    acc = term if acc is None else acc + term
            halves.append(acc)
        bias = bcf_ref[layer, :] if zero is None else bcf_ref[layer, :] + zero
        y = jnp.concatenate(halves, axis=-1) + bias
        cen = y - jnp.mean(y, axis=-1, keepdims=True)
        var = jnp.mean(cen * cen, axis=-1, keepdims=True)
        ln = cen * lax.rsqrt(var + EPS) * lng_ref[layer, :] + lnb_ref[layer, :]
        cf = ln * jax.nn.sigmoid(ln)
        zcs_ref[0, r0:r0 + rc, 0:cw] = (_rms(cf) * gc_ref[layer, :]).astype(BF16)

    def short_conv(sb):
        vwin = v_ref[...]
        wsc = wsc_ref[0]
        s = sb[mid] * (pltpu.roll(vwin, 1, 0)[mid] * wsc[0:1, :] + vwin[mid] * wsc[1:2, :]
                       + pltpu.roll(vwin, tm + 2 * HALO - 1, 0)[mid] * wsc[2:3, :])
        zcs_ref[0, :, cw:2 * cw] = (_rms(s) * gs_ref[layer, :]).astype(BF16)

    piece = lambda val: val[0:V7X_SUBLANES, 0:V7X_LANES]
    short_in = {}

    def mxu_work():
        short_in["sb"] = rdot(2)
        yield piece(short_in["sb"])
        scg = rdot(3)
        yield piece(scg)
        su = rdot(4)
        store_padded(v_ref, scg * su)
        yield piece(su)
        yield from _project(h_ref, mid, w_refs, out_refs, outs, rope, cos, sin)

    work = mxu_work()
    pending = None
    for c in range(tm // rc):
        after, pending = pending, next(work, None)
        conformer_chunk(c, after)
    for _ in work:
        pass
    short_conv(short_in["sb"])


def _mixer_in(x, mod, row, per_batch, g_pre, w, w_kv, l, tables, outs, conv_params, tm):
    b, t, d = x.shape
    (w, w_l), (w_kv, w_kv_l) = w, w_kv
    n = w.shape[2]
    rope = tables is not None
    conv = conv_params is not None
    widths = [o[2] for o in outs]
    assert t % tm == 0 and tm % HALO == 0 and all(o[1] + o[2] <= (w, w_kv)[o[0]].shape[2] for o in outs)
    tile = lambda wd: pl.BlockSpec((1, tm, wd), lambda bb, i: (bb, i, 0))
    in_specs, args = [], []
    if conv:
        prev, nxt = _halo_specs(tm, t, d)
        in_specs += [prev, tile(d), nxt]
        args += [x, x, x]
    else:
        in_specs += [tile(d)]
        args += [x]
    in_specs += [_mod_spec(row, 0, d, per_batch), _mod_spec(row, 1, d, per_batch),
                 _rows_spec(g_pre), _layer_spec(w, w_l, single_buffer=True),
                 _layer_spec(w_kv, w_kv_l, single_buffer=True)]
    args += [mod, mod, g_pre, w, w_kv]
    if rope:
        in_specs += [pl.BlockSpec((tm, V7X_LANES), lambda bb, i: (i, 0))] * 2
        args += list(tables)
    is_t = [o[3] in ("v_lo", "v_hi") for o in outs]
    out_shape = [jax.ShapeDtypeStruct((b, wd, t) if tr else (b, t, wd), BF16) for wd, tr in zip(widths, is_t)]
    out_specs = [pl.BlockSpec((1, wd, tm), lambda bb, i: (bb, 0, i)) if tr else tile(wd)
                 for wd, tr in zip(widths, is_t)]
    scratch = [pltpu.VMEM((tm, d), BF16)]
    conf_k = cw = r_col = 0
    if conv:
        w_conf_dw, b_conf_dw, ln_g, ln_b, w_sc_dw, g_group = conv_params
        conf_k, cw = w_conf_dw.shape[1:]
        r_col = n - 5 * cw
        assert w_sc_dw.shape[2] == cw and conf_k // 2 < HALO and g_group.shape[1] == ATTN_W + 2 * cw
        in_specs += [_layer_spec(w_conf_dw, l), _rows_spec(b_conf_dw), _rows_spec(ln_g), _rows_spec(ln_b),
                     _layer_spec(w_sc_dw, l),
                     _rows_spec(g_group, cols=cw, col_block=ATTN_W // cw),
                     _rows_spec(g_group, cols=cw, col_block=ATTN_W // cw + 1)]
        args += [w_conf_dw, b_conf_dw, ln_g, ln_b, w_sc_dw, g_group, g_group]
        out_shape.append(jax.ShapeDtypeStruct((b, t, 2 * cw), BF16))
        out_specs.append(tile(2 * cw))
        rows = tm + 2 * HALO
        scratch = [pltpu.VMEM((rows, d), BF16), pltpu.VMEM((rows, cw), F32), pltpu.VMEM((rows, cw), F32)]
    kern = functools.partial(_mixer_in_kernel, l=l, outs=tuple(outs), rope=rope, conv=conv, r_col=r_col,
                             tm=tm, conf_k=conf_k, cw=cw)
    return pl.pallas_call(
        kern,
        out_shape=out_shape,
        grid=(b, t // tm),
        in_specs=in_specs,
        out_specs=out_specs,
        scratch_shapes=scratch,
        compiler_params=_params("parallel", "parallel"),
        name="mixer_in",
    )(*args)


def _mix_kernel(*refs, band, t, tq, l, n_cast):
    refs = list(refs)
    if n_cast:
        cast_out = refs[-n_cast:]
        cast_in = refs[-2 - 2 * n_cast:-2 - n_cast]
        refs = refs[:-2 - 2 * n_cast] + refs[-2 - n_cast:-n_cast]
        _cast_rows(cast_in, cast_out)
    if band:
        (sink_ref, q_ref, kk_ref, vlo_ref, vhi_ref, kc_ref, vclo_ref, vchi_ref, zcs_ref, x_ref,
         gt_ref, sc2_ref, sh2_ref, ga_ref, gpost_ref, gffn_ref, wout_ref, xo_ref, h2_ref) = refs
    else:
        (sink_ref, q_ref, kc_ref, vclo_ref, vchi_ref, zcs_ref, x_ref,
         gt_ref, sc2_ref, sh2_ref, ga_ref, gpost_ref, gffn_ref, wout_ref, xo_ref, h2_ref) = refs
    blk = ATTN_BLOCK
    band_w = blk + 2 * WINDOW
    lane = lax.broadcasted_iota(jnp.int32, (1, V7X_LANES), 1)
    lo_lanes = lane < HEAD_DIM
    m_lo = jnp.where(lo_lanes, 1.0, 0.0).astype(BF16)
    m_hi = jnp.where(lo_lanes, 0.0, 1.0).astype(BF16)
    t0 = pl.program_id(1) * tq
    n_blocks = tq // blk
    groups = [(qb, h) for qb in range(n_blocks) for h in range(N_KV_HEADS)]

    def window(qb):
        q0 = t0 + qb * blk
        return q0, pl.multiple_of(jnp.clip(q0 - WINDOW, 0, t - band_w), blk)

    def scores(qb, h):
        rows = slice(qb * blk, (qb + 1) * blk)
        kv_lanes = slice(h * V7X_LANES, (h + 1) * V7X_LANES)
        qp0 = q_ref[0, rows, (2 * h) * V7X_LANES:(2 * h + 1) * V7X_LANES]
        qp1 = q_ref[0, rows, (2 * h + 1) * V7X_LANES:(2 * h + 2) * V7X_LANES]
        qs = jnp.concatenate([qp0 * m_lo, qp1 * m_lo, qp0 * m_hi, qp1 * m_hi], axis=0)
        s_ctx = _dot_nt(kc_ref[0, :, kv_lanes], qs)
        s_band = None
        if band:
            _, start = window(qb)
            s_band = _dot_nt(kk_ref[0, pl.ds(start, band_w), kv_lanes], qs)
        return s_ctx, s_band

    visible = {}
    dims_first = lax.broadcasted_iota(jnp.int32, (V7X_LANES, 1), 0) < HEAD_DIM

    def attend(qb, h, s_ctx, s_band):
        kv_rows = slice(h * V7X_LANES, (h + 1) * V7X_LANES)
        heads = (4 * h, 4 * h + 2, 4 * h + 1, 4 * h + 3)
        if band:
            q0, start = window(qb)
            if qb not in visible:
                rel = (lax.broadcasted_iota(jnp.int32, (band_w, blk), 0)
                       - lax.broadcasted_iota(jnp.int32, (band_w, blk), 1)) + (start - q0)
                visible[qb] = jnp.abs(rel) <= WINDOW
        p_ctx, p_band, snk = [], [], []
        for r, head in enumerate(heads):
            cc = slice(r * blk, (r + 1) * blk)
            sink2 = sink_ref[l * N_Q_HEADS + head] * LOG2E
            sc = s_ctx[:, cc]
            m = jnp.maximum(jnp.max(sc, axis=0, keepdims=True), sink2)
            if band:
                sb = jnp.where(visible[qb], s_band[:, cc], NEG)
                m = jnp.maximum(m, jnp.max(sb, axis=0, keepdims=True))
                p_band.append(jnp.exp2(sb - m).astype(BF16))
            p_ctx.append(jnp.exp2(sc - m).astype(BF16))
            snk.append(jnp.exp2(sink2 - m))
        o_lo = _dot(vclo_ref[0, kv_rows, :], jnp.concatenate(p_ctx[0:2], axis=1))
        o_hi = _dot(vchi_ref[0, kv_rows, :], jnp.concatenate(p_ctx[2:4], axis=1))
        if band:
            o_lo = o_lo + _dot(vlo_ref[0, kv_rows, pl.ds(start, band_w)], jnp.concatenate(p_band[0:2], axis=1))
            o_hi = o_hi + _dot(vhi_ref[0, kv_rows, pl.ds(start, band_w)], jnp.concatenate(p_band[2:4], axis=1))
        num = jnp.where(dims_first, o_lo, o_hi)
        den = jnp.where(dims_first,
                        pltpu.roll(o_lo, HEAD_DIM, 0) + jnp.concatenate(snk[0:2], axis=1),
                        pltpu.roll(o_hi, HEAD_DIM, 0) + jnp.concatenate(snk[2:4], axis=1))
        o_t = num / den
        return [o_t[:, 0:blk].T, o_t[:, blk:2 * blk].T]

    merge_rows = min(tq, MIX_MERGE_ROWS)
    blocks_per_merge = merge_rows // blk
    layer = slice(l, l + 1)
    post_gain = gt_ref[0] * gpost_ref[layer, :]
    ffn_gain = gffn_ref[layer, :] * (1.0 + sc2_ref[0])

    def merge(mi, tiles):
        rows = slice(mi * merge_rows, (mi + 1) * merge_rows)
        a = jnp.concatenate(
            [jnp.concatenate([tl for h in range(N_KV_HEADS) for tl in tiles[(qb, h)]], axis=-1)
             for qb in range(mi * blocks_per_merge, (mi + 1) * blocks_per_merge)], axis=0)
        za = (_rms(a) * ga_ref[layer, :]).astype(BF16)
        y = _dot(za, wout_ref[0, 0:ATTN_W, :]) + _dot(zcs_ref[0, rows, :], wout_ref[0, ATTN_W:, :])
        x_mid = x_ref[0, rows, :] + _rms(y) * post_gain
        xo_ref[0, rows, :] = x_mid
        h2_ref[0, rows, :] = (_rms(x_mid) * ffn_gain + sh2_ref[0]).astype(BF16)

    tiles = {}
    pending = [scores(*g) for g in groups[:MIX_AHEAD]]
    for gi, (qb, h) in enumerate(groups):
        cur = pending.pop(0)
        if gi + MIX_AHEAD < len(groups):
            pending.append(scores(*groups[gi + MIX_AHEAD]))
        tiles[(qb, h)] = attend(qb, h, *cur)
        if h == N_KV_HEADS - 1 and (qb + 1) % blocks_per_merge == 0:
            merge(qb // blocks_per_merge, tiles)


def _mix(sink, l, q, kk, vlo, vhi, kc, vclo, vchi, zcs, x, mod, row, per_batch,
         g_group, g_post, g_ffn, w_out, tq, casts=()):
    b, t, d = x.shape
    band = kk is not None
    ctx_len = kc.shape[1]
    assert t % tq == 0 and tq % min(tq, MIX_MERGE_ROWS) == 0 and MIX_MERGE_ROWS % ATTN_BLOCK == 0
    tile = lambda w: pl.BlockSpec((1, tq, w), lambda bb, i: (bb, i, 0))
    whole = lambda n, w: pl.BlockSpec((1, n, w), lambda bb, i: (bb, 0, 0))
    in_specs = [pl.BlockSpec(memory_space=pltpu.SMEM), tile(ATTN_W)]
    args = [sink, q]
    whole_t = lambda n: pl.BlockSpec((1, KV_DUP_W, n), lambda bb, i: (bb, 0, 0))
    if band:
        in_specs += [whole(t, KV_DUP_W), whole_t(t), whole_t(t)]
        args += [kk, vlo, vhi]
    in_specs += [whole(ctx_len, KV_DUP_W), whole_t(ctx_len), whole_t(ctx_len)]
    in_specs += [tile(zcs.shape[2]), tile(d),
                 _mod_spec(row, 2, d, per_batch), _mod_spec(row, 4, d, per_batch),
                 _mod_spec(row, 3, d, per_batch),
                 _rows_spec(g_group, cols=ATTN_W), _rows_spec(g_post), _rows_spec(g_ffn),
                 _layer_spec(w_out[0], w_out[1])]
    args += [kc, vclo, vchi, zcs, x, mod, mod, mod, g_group, g_post, g_ffn, w_out[0]]
    grid = (b, t // tq)
    c_in, c_args, c_shapes, c_out = _cast_slots(casts, grid)
    kern = functools.partial(_mix_kernel, band=band, t=t, tq=tq, l=l, n_cast=len(casts))
    return pl.pallas_call(
        kern,
        out_shape=[jax.ShapeDtypeStruct((b, t, d), F32), jax.ShapeDtypeStruct((b, t, d), BF16)] + c_shapes,
        grid=grid,
        in_specs=in_specs + c_in,
        out_specs=[tile(d), tile(d)] + c_out,
        compiler_params=_params("arbitrary", "arbitrary") if casts else _params("parallel", "parallel"),
        name="mix",
    )(*args, *c_args)


def _ffn_kernel(*refs, l, tm, d_ff, tf, n_cast):
    hp_ref, h_ref, hn_ref, x_ref, gt_ref, gpost_ref, wup_ref, wdw_ref, wdn_ref = refs[:9]
    cast_in, o_ref, cast_out = refs[9:9 + n_cast], refs[9 + n_cast], refs[10 + n_cast:10 + 2 * n_cast]
    act_ref, = refs[10 + 2 * n_cast:]
    _cast_rows(cast_in, cast_out)
    i = pl.program_id(1)
    rows = tm + 2 * HALO
    zeros = jnp.zeros((HALO, h_ref.shape[2]), BF16)
    hc = jnp.concatenate([jnp.where(i > 0, hp_ref[0], zeros), h_ref[0],
                          jnp.where(i < pl.num_programs(1) - 1, hn_ref[0], zeros)], axis=0)
    mid = slice(HALO, HALO + tm)

    def conv3(u, w):
        return (pltpu.roll(u, 1, 0)[mid] * w[0:1, :] + u[mid] * w[1:2, :]
                + pltpu.roll(u, rows - 1, 0)[mid] * w[2:3, :])

    n_chunks = d_ff // tf

    def up(c):
        gcols = slice(c * tf, (c + 1) * tf)
        vcols = slice(d_ff + c * tf, d_ff + (c + 1) * tf)
        return _dot(hc, wup_ref[0, :, gcols]), _dot(hc, wup_ref[0, :, vcols])

    nxt = up(0)
    for c in range(n_chunks):
        ug, uv = nxt
        if c + 1 < n_chunks:
            nxt = up(c + 1)
        gcols = slice(c * tf, (c + 1) * tf)
        vcols = slice(d_ff + c * tf, d_ff + (c + 1) * tf)
        gate = conv3(ug, wdw_ref[0, :, gcols])
        val = conv3(uv, wdw_ref[0, :, vcols])
        act_ref[:, gcols] = (gate * jax.nn.sigmoid(gate) * val).astype(BF16)
    y = _dot(act_ref[...], wdn_ref[0])
    o_ref[0] = x_ref[0] + (gt_ref[0] * gpost_ref[l:l + 1, :]) * _rms(y)


def _ffn(h2, x, mod, row, per_batch, l, g_post, w_up, w_dw, w_down, tm, casts=()):
    b, t, d = x.shape
    (w_up, w_up_l), (w_down, w_down_l) = w_up, w_down
    d_ff = w_down.shape[1]
    assert t % tm == 0 and tm % HALO == 0 and d_ff % FFN_TF == 0
    prev, nxt = _halo_specs(tm, t, d)
    tile = pl.BlockSpec((1, tm, d), lambda bb, i: (bb, i, 0))
    grid = (b, t // tm)
    c_in, c_args, c_shapes, c_out = _cast_slots(casts, grid)
    kern = functools.partial(_ffn_kernel, l=l, tm=tm, d_ff=d_ff, tf=FFN_TF, n_cast=len(casts))
    out = pl.pallas_call(
        kern,
        out_shape=[jax.ShapeDtypeStruct((b, t, d), F32)] + c_shapes,
        grid=grid,
        in_specs=[prev, tile, nxt, tile, _mod_spec(row, 5, d, per_batch), _rows_spec(g_post),
                  _layer_spec(w_up, w_up_l, single_buffer=True), _layer_spec(w_dw, l),
                  _layer_spec(w_down, w_down_l, single_buffer=True)] + c_in,
        out_specs=[tile] + c_out,
        scratch_shapes=[pltpu.VMEM((tm, d_ff), BF16)],
        compiler_params=_params("arbitrary", "arbitrary") if casts else _params("parallel", "parallel"),
        name="conv_ffn",
    )(h2, h2, h2, x, mod, g_post, w_up, w_dw, w_down, *c_args)
    return out if casts else out[0]


def _rope_tables(n_tokens):
    lane = np.arange(V7X_LANES)
    row_axis = ((lane % HEAD_DIM) // (2 * ROPE_FREQS)) == 0
    sign = np.where((lane % (2 * ROPE_FREQS)) < ROPE_FREQS, -1.0, 1.0).astype(np.float32)
    inv = ROPE_THETA ** (-jnp.arange(ROPE_FREQS, dtype=F32) / ROPE_FREQS)
    tok = jnp.arange(n_tokens)
    pos = jnp.where(jnp.asarray(row_axis)[None, :], (tok // GRID_W)[:, None], (tok % GRID_W)[:, None])
    ang = pos.astype(F32) * jnp.tile(inv, V7X_LANES // ROPE_FREQS)[None, :]
    return jnp.cos(ang), jnp.sin(ang) * sign[None, :]


def _kv_dup_weights(w_in):
    depth, d, _ = w_in.shape
    heads = w_in[:, :, ATTN_W:ATTN_W + 2 * KV_W].astype(BF16).reshape(depth, d, 2 * N_KV_HEADS, 1, HEAD_DIM)
    return jnp.broadcast_to(heads, (depth, d, 2 * N_KV_HEADS, 2, HEAD_DIM)).reshape(depth, d, 2 * KV_DUP_W)


def kernel(x, c, ctx, c_ctx, w_ada, b_ada, g_pre_mix, g_post_mix, g_pre_ffn, g_post_ffn, w_in, sink,
           w_conf_dw, b_conf_dw, conf_ln_g, conf_ln_b, w_sc_dw, g_group, w_out, w_up, w_ffn_dw, w_down):
    batch, seq, d = x.shape
    depth = w_in.shape[0]
    ctx_len = ctx.shape[1]
    assert batch < MOD_ROWS

    cc = jnp.concatenate([c, c_ctx[None, :], jnp.zeros((MOD_ROWS - batch - 1, d), F32)], axis=0)
    mod, w_in_first, w_out_first = _modulation(cc, w_ada, b_ada, ((w_in, 0), (w_out, 0)))
    mod = mod.reshape(depth * MOD_ROWS, 1, N_MOD * d)
    tables = _rope_tables(seq)
    sink_flat = sink.reshape(-1).astype(F32)
    w_kv = _kv_dup_weights(w_in)
    conv_params = (w_conf_dw, b_conf_dw, conf_ln_g, conf_ln_b, w_sc_dw, g_group)

    kv_outs = [(1, 0, KV_DUP_W, "k"), (1, KV_DUP_W, KV_DUP_W, "v_lo"), (1, KV_DUP_W, KV_DUP_W, "v_hi")]
    all_outs = [(0, 0, ATTN_W, "q")] + kv_outs

    w_in_b, w_out_b = (w_in_first, 0), (w_out_first, 0)

    for l in range(depth):
        update_ctx = l < depth - 1
        row = l * MOD_ROWS
        q, kk, vlo, vhi, zcs = _mixer_in(x, mod, row, True, g_pre_mix, w_in_b, (w_kv, l), l, tables, all_outs,
                                         conv_params, IN_TM)
        if update_ctx:
            qc, kc, vclo, vchi, zcs_c = _mixer_in(ctx, mod, row + batch, False, g_pre_mix, w_in_b, (w_kv, l), l,
                                                  None, all_outs, conv_params, ctx_len)
        else:
            kc, vclo, vchi = _mixer_in(ctx, mod, row + batch, False, g_pre_mix, w_in_b, (w_kv, l), l, None,
                                       kv_outs, None, ctx_len)
        x_mid, h2, w_up_l, w_down_l = _mix(sink_flat, l, q, kk, vlo, vhi, kc, vclo, vchi, zcs, x, mod, row, True,
                                           g_group, g_post_mix, g_pre_ffn, w_out_b, MIX_TQ,
                                           casts=((w_up, l), (w_down, l)))
        w_up_b, w_down_b = (w_up_l, 0), (w_down_l, 0)
        if update_ctx:
            ctx_mid, hc2 = _mix(sink_flat, l, qc, None, None, None, kc, vclo, vchi, zcs_c, ctx, mod,
                                row + batch, False, g_group, g_post_mix, g_pre_ffn, w_out_b, ctx_len)
            ctx = _ffn(hc2, ctx_mid, mod, row + batch, False, l, g_post_ffn, w_up_b, w_ffn_dw, w_down_b,
                       ctx_len)
        if l + 1 < depth:
            x, w_in_next, w_out_next = _ffn(h2, x_mid, mod, row, True, l, g_post_ffn, w_up_b, w_ffn_dw, w_down_b,
                                            FFN_TM, casts=((w_in, l + 1), (w_out, l + 1)))
            w_in_b, w_out_b = (w_in_next, 0), (w_out_next, 0)
        else:
            x = _ffn(h2, x_mid, mod, row, True, l, g_post_ffn, w_up_b, w_ffn_dw, w_down_b, FFN_TM)
    return x
```

```python
import functools

import numpy as np
import jax
import jax.numpy as jnp
from jax import lax
from jax.experimental import pallas as pl
from jax.experimental.pallas import tpu as pltpu

F32 = jnp.float32
BF16 = jnp.bfloat16

V7X_LANES = 128
V7X_SUBLANES = 8
V7X_BF16_SUBLANES = 16
V7X_VMEM_LIMIT_BYTES = 56 * 1024 * 1024

HEAD_DIM = 64
N_Q_HEADS = 8
N_KV_HEADS = 2
GRID_W = 64
WINDOW = 128
ATTN_BLOCK = 128
ROPE_FREQS = HEAD_DIM // 4
ROPE_THETA = 10000.0
EPS = 1e-6
NEG = -1e30
LOG2E = 1.4426950408889634

ATTN_W = N_Q_HEADS * HEAD_DIM
KV_W = N_KV_HEADS * HEAD_DIM
KV_DUP_W = 2 * KV_W
N_MOD = 6
MOD_ROWS = 16

IN_TM = 512
MIX_TQ = 512
MIX_MERGE_ROWS = 256
MIX_AHEAD = 2
FFN_TM = 512
FFN_TF = 256
HALO = V7X_BF16_SUBLANES
CONV_RC = 64
PROJ_CHUNK = 2 * V7X_LANES
MOD_TN = 1536


def _rms(x, eps=EPS):
    return x * lax.rsqrt(jnp.mean(x * x, axis=-1, keepdims=True) + eps)


def _dot(a, b):
    return jnp.dot(a, b, preferred_element_type=F32)


def _dot_nt(a, b):
    return lax.dot_general(a, b, (((1,), (1,)), ((), ())), preferred_element_type=F32)


def _params(*sem):
    return pltpu.CompilerParams(dimension_semantics=sem, vmem_limit_bytes=V7X_VMEM_LIMIT_BYTES)


def _layer_spec(arr, l, cols=None, col_block=0, single_buffer=False):
    _, rows, width = arr.shape
    width = cols or width
    mode = dict(pipeline_mode=pl.Buffered(1)) if single_buffer else {}
    return pl.BlockSpec((1, rows, width), lambda *_: (l, 0, col_block), **mode)


def _rows_spec(arr, cols=None, col_block=0):
    depth, width = arr.shape
    return pl.BlockSpec((depth, cols or width), lambda *_: (0, col_block))


def _mod_spec(row, k, d, per_batch):
    if per_batch:
        return pl.BlockSpec((1, 1, d), lambda b, i: (row + b, 0, k))
    return pl.BlockSpec((1, 1, d), lambda b, i: (row, 0, k))


def _cast_slots(weights, grid):
    steps = grid[0] * grid[1]
    in_specs, args, out_shapes, out_specs = [], [], [], []
    for arr, layer in weights:
        _, r, c = arr.shape
        span = 1
        while r % (steps // span) or (r // (steps // span)) % V7X_BF16_SUBLANES:
            span *= 2
            assert span <= steps, (r, steps)
        rows = r // (steps // span)
        in_specs.append(pl.BlockSpec((1, rows, c), lambda bb, i, layer=layer, span=span:
                                     (layer, (bb * grid[1] + i) // span, 0)))
        out_specs.append(pl.BlockSpec((1, rows, c), lambda bb, i, span=span: (0, (bb * grid[1] + i) // span, 0)))
        out_shapes.append(jax.ShapeDtypeStruct((1, r, c), BF16))
        args.append(arr)
    return in_specs, args, out_shapes, out_specs


def _cast_rows(cast_in, cast_out):
    for src, dst in zip(cast_in, cast_out):
        dst[...] = src[...].astype(BF16)


def _halo_specs(tm, t, d):
    per_tile = tm // HALO
    last = t // HALO - 1
    return (pl.BlockSpec((1, HALO, d), lambda bb, i: (bb, jnp.maximum(i * per_tile - 1, 0), 0)),
            pl.BlockSpec((1, HALO, d), lambda bb, i: (bb, jnp.minimum((i + 1) * per_tile, last), 0)))


def _mod_kernel(*refs, n_cast):
    c_ref, w_ref, b_ref = refs[:3]
    cast_in, o_ref, cast_out = refs[3:3 + n_cast], refs[3 + n_cast], refs[4 + n_cast:]
    _cast_rows(cast_in, cast_out)
    c = c_ref[...]
    a = c * jax.nn.sigmoid(c)
    w = w_ref[0]
    a_hi = a.astype(BF16)
    a_lo = (a - a_hi.astype(F32)).astype(BF16)
    w_hi = w.astype(BF16)
    w_lo = (w - w_hi.astype(F32)).astype(BF16)
    hi_lo = _dot(jnp.concatenate([a_hi, a_lo], axis=0), w_hi)
    acc = hi_lo[0:MOD_ROWS] + hi_lo[MOD_ROWS:2 * MOD_ROWS] + _dot(a_hi, w_lo)
    o_ref[0] = acc + b_ref[0]


def _modulation(cc, w_ada, b_ada, casts):
    depth, d, n = w_ada.shape
    grid = (depth, n // MOD_TN)
    c_in, c_args, c_shapes, c_out = _cast_slots(casts, grid)
    return pl.pallas_call(
        functools.partial(_mod_kernel, n_cast=len(casts)),
        out_shape=[jax.ShapeDtypeStruct((depth, MOD_ROWS, n), F32)] + c_shapes,
        grid=grid,
        in_specs=[
            pl.BlockSpec((MOD_ROWS, d), lambda l, j: (0, 0)),
            pl.BlockSpec((1, d, MOD_TN), lambda l, j: (l, 0, j)),
            pl.BlockSpec((1, 1, MOD_TN), lambda l, j: (l, 0, j)),
        ] + c_in,
        out_specs=[pl.BlockSpec((1, MOD_ROWS, MOD_TN), lambda l, j: (l, 0, j))] + c_out,
        compiler_params=_params("arbitrary", "arbitrary"),
        name="modulation",
    )(cc, w_ada, b_ada.reshape(depth, 1, n), *c_args)


def _project(h_ref, rows, w_refs, out_refs, outs, rope, cos, sin):
    lane = lax.broadcasted_iota(jnp.int32, (1, V7X_LANES), 1)
    dims_first = lax.broadcasted_iota(jnp.int32, (V7X_LANES, 1), 0) < HEAD_DIM
    first_half = (lane % (2 * ROPE_FREQS)) < ROPE_FREQS
    chunks = {}
    for o_ref, (wi, col, width, kind) in zip(out_refs, outs):
        for c0 in range(0, width, PROJ_CHUNK):
            chunks.setdefault((wi, col + c0, col + c0 + PROJ_CHUNK), []).append((o_ref, c0, kind))
    prev = None
    for cols, users in list(chunks.items()) + [(None, None)]:
        cur = (_dot(h_ref[rows, :], w_refs[cols[0]][0, :, cols[1]:cols[2]]), users) if users else None
        if prev is not None:
            p, p_users = prev
            transposed = {}
            for o_ref, c0, kind in p_users:
                tiles = []
                for j in range(PROJ_CHUNK // V7X_LANES):
                    t = p[:, j * V7X_LANES:(j + 1) * V7X_LANES]
                    if kind in ("v_lo", "v_hi"):
                        if j not in transposed:
                            transposed[j] = t.T
                        keep = dims_first if kind == "v_lo" else jnp.logical_not(dims_first)
                        t_t = jnp.where(keep, transposed[j], 1.0)
                        o_ref[0, c0 + j * V7X_LANES:c0 + (j + 1) * V7X_LANES, :] = t_t.astype(BF16)
                        continue
                    if rope:
                        partner = jnp.where(first_half,
                                            pltpu.roll(t, V7X_LANES - ROPE_FREQS, 1),
                                            pltpu.roll(t, ROPE_FREQS, 1))
                        t = t * cos + partner * sin
                    if kind == "q":
                        t = t * (HEAD_DIM ** -0.5 * LOG2E)
                    tiles.append(t.astype(BF16))
                if tiles:
                    o_ref[0, :, c0:c0 + PROJ_CHUNK] = jnp.concatenate(tiles, axis=-1)
        prev = cur
        yield None if cur is None else cur[0][0:V7X_SUBLANES, 0:V7X_LANES]


def _mixer_in_kernel(*refs, l, outs, rope, conv, r_col, tm, conf_k, cw):
    it = iter(refs)
    if conv:
        xp_ref, x_ref, xn_ref = next(it), next(it), next(it)
    else:
        x_ref = next(it)
    sh_ref, sc_ref, g_ref, w_ref, wkv_ref = next(it), next(it), next(it), next(it), next(it)
    w_refs = (w_ref, wkv_ref)
    layer = slice(l, l + 1)
    cos = sin = None
    if rope:
        cos, sin = next(it)[...], next(it)[...]
    if conv:
        wcf_ref, bcf_ref, lng_ref, lnb_ref, wsc_ref, gc_ref, gs_ref = (next(it) for _ in range(7))
    out_refs = [next(it) for _ in outs]
    if conv:
        zcs_ref, h_ref, u_ref, v_ref = next(it), next(it), next(it), next(it)
    else:
        h_ref = next(it)

    gain = g_ref[layer, :] * (1.0 + sc_ref[0])
    shift = sh_ref[0]

    def modulate(xv):
        return (_rms(xv) * gain + shift).astype(BF16)

    if not conv:
        h_ref[...] = modulate(x_ref[0])
        for _ in _project(h_ref, slice(0, tm), w_refs, out_refs, outs, rope, cos, sin):
            pass
        return

    i = pl.program_id(1)
    top_ok = (i > 0).astype(F32)
    bot_ok = (i < pl.num_programs(1) - 1).astype(F32)
    mid = slice(HALO, HALO + tm)
    top = slice(0, HALO)
    bot = slice(HALO + tm, HALO + tm + HALO)
    h_ref[mid, :] = modulate(x_ref[0])
    h_ref[top, :] = modulate(xp_ref[0])
    h_ref[bot, :] = modulate(xn_ref[0])

    def rdot(k):
        return _dot(h_ref[...], w_ref[0, :, r_col + k * cw:r_col + (k + 1) * cw])

    def store_padded(ref, val):
        ref[mid, :] = val[mid]
        ref[top, :] = val[top] * top_ok
        ref[bot, :] = val[bot] * bot_ok

    cv, cg = rdot(0), rdot(1)
    store_padded(u_ref, cv * jax.nn.sigmoid(cg))

    rc = min(CONV_RC, tm)
    win = rc + 2 * HALO
    pad_c = conf_k // 2

    def conformer_chunk(c, after):
        r0 = c * rc
        zero = None
        if after is not None:
            half = jnp.uint32(16)
            bits = lax.shift_right_logical(lax.shift_right_logical(pltpu.bitcast(after, jnp.uint32), half), half)
            zero = pltpu.bitcast(bits, F32)[0:1, 0:1]
        halves = []
        for hh in range(cw // V7X_LANES):
            lanes = slice(hh * V7X_LANES, (hh + 1) * V7X_LANES)
            wdw = wcf_ref[0, :, lanes]
            window = u_ref[r0:r0 + win, lanes]
            acc = None
            for res in range(V7X_SUBLANES):
                shifted = window if res == 0 else pltpu.roll(window, win - res, 0)
                for k in range(conf_k):
                    off = HALO - pad_c + k
                    if off % V7X_SUBLANES != res:
                        continue
                    term = shifted[off - res:off - res + rc, :] * wdw[k:k + 1, :]
                    acc = term if acc is None else acc + term
            halves.append(acc)
        bias = bcf_ref[layer, :] if zero is None else bcf_ref[layer, :] + zero
        y = jnp.concatenate(halves, axis=-1) + bias
        cen = y - jnp.mean(y, axis=-1, keepdims=True)
        var = jnp.mean(cen * cen, axis=-1, keepdims=True)
        ln = cen * lax.rsqrt(var + EPS) * lng_ref[layer, :] + lnb_ref[layer, :]
        cf = ln * jax.nn.sigmoid(ln)
        zcs_ref[0, r0:r0 + rc, 0:cw] = (_rms(cf) * gc_ref[layer, :]).astype(BF16)

    def short_conv(sb):
        vwin = v_ref[...]
        wsc = wsc_ref[0]
        s = sb[mid] * (pltpu.roll(vwin, 1, 0)[mid] * wsc[0:1, :] + vwin[mid] * wsc[1:2, :]
                       + pltpu.roll(vwin, tm + 2 * HALO - 1, 0)[mid] * wsc[2:3, :])
        zcs_ref[0, :, cw:2 * cw] = (_rms(s) * gs_ref[layer, :]).astype(BF16)

    piece = lambda val: val[0:V7X_SUBLANES, 0:V7X_LANES]
    short_in = {}

    def mxu_work():
        short_in["sb"] = rdot(2)
        yield piece(short_in["sb"])
        scg = rdot(3)
        yield piece(scg)
        su = rdot(4)
        store_padded(v_ref, scg * su)
        yield piece(su)
        yield from _project(h_ref, mid, w_refs, out_refs, outs, rope, cos, sin)

    work = mxu_work()
    pending = None
    for c in range(tm // rc):
        after, pending = pending, next(work, None)
        conformer_chunk(c, after)
    for _ in work:
        pass
    short_conv(short_in["sb"])


def _mixer_in(x, mod, row, per_batch, g_pre, w, w_kv, l, tables, outs, conv_params, tm):
    b, t, d = x.shape
    (w, w_l), (w_kv, w_kv_l) = w, w_kv
    n = w.shape[2]
    rope = tables is not None
    conv = conv_params is not None
    widths = [o[2] for o in outs]
    assert t % tm == 0 and tm % HALO == 0 and all(o[1] + o[2] <= (w, w_kv)[o[0]].shape[2] for o in outs)
    tile = lambda wd: pl.BlockSpec((1, tm, wd), lambda bb, i: (bb, i, 0))
    in_specs, args = [], []
    if conv:
        prev, nxt = _halo_specs(tm, t, d)
        in_specs += [prev, tile(d), nxt]
        args += [x, x, x]
    else:
        in_specs += [tile(d)]
        args += [x]
    in_specs += [_mod_spec(row, 0, d, per_batch), _mod_spec(row, 1, d, per_batch),
                 _rows_spec(g_pre), _layer_spec(w, w_l, single_buffer=True),
                 _layer_spec(w_kv, w_kv_l, single_buffer=True)]
    args += [mod, mod, g_pre, w, w_kv]
    if rope:
        in_specs += [pl.BlockSpec((tm, V7X_LANES), lambda bb, i: (i, 0))] * 2
        args += list(tables)
    is_t = [o[3] in ("v_lo", "v_hi") for o in outs]
    out_shape = [jax.ShapeDtypeStruct((b, wd, t) if tr else (b, t, wd), BF16) for wd, tr in zip(widths, is_t)]
    out_specs = [pl.BlockSpec((1, wd, tm), lambda bb, i: (bb, 0, i)) if tr else tile(wd)
                 for wd, tr in zip(widths, is_t)]
    scratch = [pltpu.VMEM((tm, d), BF16)]
    conf_k = cw = r_col = 0
    if conv:
        w_conf_dw, b_conf_dw, ln_g, ln_b, w_sc_dw, g_group = conv_params
        conf_k, cw = w_conf_dw.shape[1:]
        r_col = n - 5 * cw
        assert w_sc_dw.shape[2] == cw and conf_k // 2 < HALO and g_group.shape[1] == ATTN_W + 2 * cw
        in_specs += [_layer_spec(w_conf_dw, l), _rows_spec(b_conf_dw), _rows_spec(ln_g), _rows_spec(ln_b),
                     _layer_spec(w_sc_dw, l),
                     _rows_spec(g_group, cols=cw, col_block=ATTN_W // cw),
                     _rows_spec(g_group, cols=cw, col_block=ATTN_W // cw + 1)]
        args += [w_conf_dw, b_conf_dw, ln_g, ln_b, w_sc_dw, g_group, g_group]
        out_shape.append(jax.ShapeDtypeStruct((b, t, 2 * cw), BF16))
        out_specs.append(tile(2 * cw))
        rows = tm + 2 * HALO
        scratch = [pltpu.VMEM((rows, d), BF16), pltpu.VMEM((rows, cw), F32), pltpu.VMEM((rows, cw), F32)]
    kern = functools.partial(_mixer_in_kernel, l=l, outs=tuple(outs), rope=rope, conv=conv, r_col=r_col,
                             tm=tm, conf_k=conf_k, cw=cw)
    return pl.pallas_call(
        kern,
        out_shape=out_shape,
        grid=(b, t // tm),
        in_specs=in_specs,
        out_specs=out_specs,
        scratch_shapes=scratch,
        compiler_params=_params("parallel", "parallel"),
        name="mixer_in",
    )(*args)


def _mix_kernel(*refs, band, t, tq, l, n_cast):
    refs = list(refs)
    if n_cast:
        cast_out = refs[-n_cast:]
        cast_in = refs[-2 - 2 * n_cast:-2 - n_cast]
        refs = refs[:-2 - 2 * n_cast] + refs[-2 - n_cast:-n_cast]
        _cast_rows(cast_in, cast_out)
    if band:
        (sink_ref, q_ref, kk_ref, vlo_ref, vhi_ref, kc_ref, vclo_ref, vchi_ref, zcs_ref, x_ref,
         gt_ref, sc2_ref, sh2_ref, ga_ref, gpost_ref, gffn_ref, wout_ref, xo_ref, h2_ref) = refs
    else:
        (sink_ref, q_ref, kc_ref, vclo_ref, vchi_ref, zcs_ref, x_ref,
         gt_ref, sc2_ref, sh2_ref, ga_ref, gpost_ref, gffn_ref, wout_ref, xo_ref, h2_ref) = refs
    blk = ATTN_BLOCK
    band_w = blk + 2 * WINDOW
    lane = lax.broadcasted_iota(jnp.int32, (1, V7X_LANES), 1)
    lo_lanes = lane < HEAD_DIM
    m_lo = jnp.where(lo_lanes, 1.0, 0.0).astype(BF16)
    m_hi = jnp.where(lo_lanes, 0.0, 1.0).astype(BF16)
    t0 = pl.program_id(1) * tq
    n_blocks = tq // blk
    groups = [(qb, h) for qb in range(n_blocks) for h in range(N_KV_HEADS)]

    def window(qb):
        q0 = t0 + qb * blk
        return q0, pl.multiple_of(jnp.clip(q0 - WINDOW, 0, t - band_w), blk)

    def scores(qb, h):
        rows = slice(qb * blk, (qb + 1) * blk)
        kv_lanes = slice(h * V7X_LANES, (h + 1) * V7X_LANES)
        qp0 = q_ref[0, rows, (2 * h) * V7X_LANES:(2 * h + 1) * V7X_LANES]
        qp1 = q_ref[0, rows, (2 * h + 1) * V7X_LANES:(2 * h + 2) * V7X_LANES]
        qs = jnp.concatenate([qp0 * m_lo, qp1 * m_lo, qp0 * m_hi, qp1 * m_hi], axis=0)
        s_ctx = _dot_nt(kc_ref[0, :, kv_lanes], qs)
        s_band = None
        if band:
            _, start = window(qb)
            s_band = _dot_nt(kk_ref[0, pl.ds(start, band_w), kv_lanes], qs)
        return s_ctx, s_band

    visible = {}
    dims_first = lax.broadcasted_iota(jnp.int32, (V7X_LANES, 1), 0) < HEAD_DIM

    def attend(qb, h, s_ctx, s_band):
        kv_rows = slice(h * V7X_LANES, (h + 1) * V7X_LANES)
        heads = (4 * h, 4 * h + 2, 4 * h + 1, 4 * h + 3)
        if band:
            q0, start = window(qb)
            if qb not in visible:
                rel = (lax.broadcasted_iota(jnp.int32, (band_w, blk), 0)
                       - lax.broadcasted_iota(jnp.int32, (band_w, blk), 1)) + (start - q0)
                visible[qb] = jnp.abs(rel) <= WINDOW
        p_ctx, p_band, snk = [], [], []
        for r, head in enumerate(heads):
            cc = slice(r * blk, (r + 1) * blk)
            sink2 = sink_ref[l * N_Q_HEADS + head] * LOG2E
            sc = s_ctx[:, cc]
            m = jnp.maximum(jnp.max(sc, axis=0, keepdims=True), sink2)
            if band:
                sb = jnp.where(visible[qb], s_band[:, cc], NEG)
                m = jnp.maximum(m, jnp.max(sb, axis=0, keepdims=True))
                p_band.append(jnp.exp2(sb - m).astype(BF16))
            p_ctx.append(jnp.exp2(sc - m).astype(BF16))
            snk.append(jnp.exp2(sink2 - m))
        o_lo = _dot(vclo_ref[0, kv_rows, :], jnp.concatenate(p_ctx[0:2], axis=1))
        o_hi = _dot(vchi_ref[0, kv_rows, :], jnp.concatenate(p_ctx[2:4], axis=1))
        if band:
            o_lo = o_lo + _dot(vlo_ref[0, kv_rows, pl.ds(start, band_w)], jnp.concatenate(p_band[0:2], axis=1))
            o_hi = o_hi + _dot(vhi_ref[0, kv_rows, pl.ds(start, band_w)], jnp.concatenate(p_band[2:4], axis=1))
        num = jnp.where(dims_first, o_lo, o_hi)
        den = jnp.where(dims_first,
                        pltpu.roll(o_lo, HEAD_DIM, 0) + jnp.concatenate(snk[0:2], axis=1),
                        pltpu.roll(o_hi, HEAD_DIM, 0) + jnp.concatenate(snk[2:4], axis=1))
        o_t = num / den
        return [o_t[:, 0:blk].T, o_t[:, blk:2 * blk].T]

    merge_rows = min(tq, MIX_MERGE_ROWS)
    blocks_per_merge = merge_rows // blk
    layer = slice(l, l + 1)
    post_gain = gt_ref[0] * gpost_ref[layer, :]
    ffn_gain = gffn_ref[layer, :] * (1.0 + sc2_ref[0])

    def merge(mi, tiles):
        rows = slice(mi * merge_rows, (mi + 1) * merge_rows)
        a = jnp.concatenate(
            [jnp.concatenate([tl for h in range(N_KV_HEADS) for tl in tiles[(qb, h)]], axis=-1)
             for qb in range(mi * blocks_per_merge, (mi + 1) * blocks_per_merge)], axis=0)
        za = (_rms(a) * ga_ref[layer, :]).astype(BF16)
        y = _dot(za, wout_ref[0, 0:ATTN_W, :]) + _dot(zcs_ref[0, rows, :], wout_ref[0, ATTN_W:, :])
        x_mid = x_ref[0, rows, :] + _rms(y) * post_gain
        xo_ref[0, rows, :] = x_mid
        h2_ref[0, rows, :] = (_rms(x_mid) * ffn_gain + sh2_ref[0]).astype(BF16)

    tiles = {}
    pending = [scores(*g) for g in groups[:MIX_AHEAD]]
    for gi, (qb, h) in enumerate(groups):
        cur = pending.pop(0)
        if gi + MIX_AHEAD < len(groups):
            pending.append(scores(*groups[gi + MIX_AHEAD]))
        tiles[(qb, h)] = attend(qb, h, *cur)
        if h == N_KV_HEADS - 1 and (qb + 1) % blocks_per_merge == 0:
            merge(qb // blocks_per_merge, tiles)


def _mix(sink, l, q, kk, vlo, vhi, kc, vclo, vchi, zcs, x, mod, row, per_batch,
         g_group, g_post, g_ffn, w_out, tq, casts=()):
    b, t, d = x.shape
    band = kk is not None
    ctx_len = kc.shape[1]
    assert t % tq == 0 and tq % min(tq, MIX_MERGE_ROWS) == 0 and MIX_MERGE_ROWS % ATTN_BLOCK == 0
    tile = lambda w: pl.BlockSpec((1, tq, w), lambda bb, i: (bb, i, 0))
    whole = lambda n, w: pl.BlockSpec((1, n, w), lambda bb, i: (bb, 0, 0))
    in_specs = [pl.BlockSpec(memory_space=pltpu.SMEM), tile(ATTN_W)]
    args = [sink, q]
    whole_t = lambda n: pl.BlockSpec((1, KV_DUP_W, n), lambda bb, i: (bb, 0, 0))
    if band:
        in_specs += [whole(t, KV_DUP_W), whole_t(t), whole_t(t)]
        args += [kk, vlo, vhi]
    in_specs += [whole(ctx_len, KV_DUP_W), whole_t(ctx_len), whole_t(ctx_len)]
    in_specs += [tile(zcs.shape[2]), tile(d),
                 _mod_spec(row, 2, d, per_batch), _mod_spec(row, 4, d, per_batch),
                 _mod_spec(row, 3, d, per_batch),
                 _rows_spec(g_group, cols=ATTN_W), _rows_spec(g_post), _rows_spec(g_ffn),
                 _layer_spec(w_out[0], w_out[1])]
    args += [kc, vclo, vchi, zcs, x, mod, mod, mod, g_group, g_post, g_ffn, w_out[0]]
    grid = (b, t // tq)
    c_in, c_args, c_shapes, c_out = _cast_slots(casts, grid)
    kern = functools.partial(_mix_kernel, band=band, t=t, tq=tq, l=l, n_cast=len(casts))
    return pl.pallas_call(
        kern,
        out_shape=[jax.ShapeDtypeStruct((b, t, d), F32), jax.ShapeDtypeStruct((b, t, d), BF16)] + c_shapes,
        grid=grid,
        in_specs=in_specs + c_in,
        out_specs=[tile(d), tile(d)] + c_out,
        compiler_params=_params("arbitrary", "arbitrary") if casts else _params("parallel", "parallel"),
        name="mix",
    )(*args, *c_args)


def _ffn_kernel(*refs, l, tm, d_ff, tf, n_cast):
    hp_ref, h_ref, hn_ref, x_ref, gt_ref, gpost_ref, wup_ref, wdw_ref, wdn_ref = refs[:9]
    cast_in, o_ref, cast_out = refs[9:9 + n_cast], refs[9 + n_cast], refs[10 + n_cast:10 + 2 * n_cast]
    act_ref, = refs[10 + 2 * n_cast:]
    _cast_rows(cast_in, cast_out)
    i = pl.program_id(1)
    rows = tm + 2 * HALO
    zeros = jnp.zeros((HALO, h_ref.shape[2]), BF16)
    hc = jnp.concatenate([jnp.where(i > 0, hp_ref[0], zeros), h_ref[0],
                          jnp.where(i < pl.num_programs(1) - 1, hn_ref[0], zeros)], axis=0)
    mid = slice(HALO, HALO + tm)

    def conv3(u, w):
        return (pltpu.roll(u, 1, 0)[mid] * w[0:1, :] + u[mid] * w[1:2, :]
                + pltpu.roll(u, rows - 1, 0)[mid] * w[2:3, :])

    n_chunks = d_ff // tf

    def up(c):
        gcols = slice(c * tf, (c + 1) * tf)
        vcols = slice(d_ff + c * tf, d_ff + (c + 1) * tf)
        return _dot(hc, wup_ref[0, :, gcols]), _dot(hc, wup_ref[0, :, vcols])

    nxt = up(0)
    for c in range(n_chunks):
        ug, uv = nxt
        if c + 1 < n_chunks:
            nxt = up(c + 1)
        gcols = slice(c * tf, (c + 1) * tf)
        vcols = slice(d_ff + c * tf, d_ff + (c + 1) * tf)
        for hh in range(tf // V7X_LANES):
            lanes = slice(hh * V7X_LANES, (hh + 1) * V7X_LANES)
            g0, v0 = c * tf + hh * V7X_LANES, d_ff + c * tf + hh * V7X_LANES
            gate = conv3(ug[:, lanes], wdw_ref[0, :, g0:g0 + V7X_LANES])
            val = conv3(uv[:, lanes], wdw_ref[0, :, v0:v0 + V7X_LANES])
            act_ref[:, g0:g0 + V7X_LANES] = (gate * jax.nn.sigmoid(gate) * val).astype(BF16)
    y = _dot(act_ref[...], wdn_ref[0])
    o_ref[0] = x_ref[0] + (gt_ref[0] * gpost_ref[l:l + 1, :]) * _rms(y)


def _ffn(h2, x, mod, row, per_batch, l, g_post, w_up, w_dw, w_down, tm, casts=()):
    b, t, d = x.shape
    (w_up, w_up_l), (w_down, w_down_l) = w_up, w_down
    d_ff = w_down.shape[1]
    assert t % tm == 0 and tm % HALO == 0 and d_ff % FFN_TF == 0
    prev, nxt = _halo_specs(tm, t, d)
    tile = pl.BlockSpec((1, tm, d), lambda bb, i: (bb, i, 0))
    grid = (b, t // tm)
    c_in, c_args, c_shapes, c_out = _cast_slots(casts, grid)
    kern = functools.partial(_ffn_kernel, l=l, tm=tm, d_ff=d_ff, tf=FFN_TF, n_cast=len(casts))
    out = pl.pallas_call(
        kern,
        out_shape=[jax.ShapeDtypeStruct((b, t, d), F32)] + c_shapes,
        grid=grid,
        in_specs=[prev, tile, nxt, tile, _mod_spec(row, 5, d, per_batch), _rows_spec(g_post),
                  _layer_spec(w_up, w_up_l, single_buffer=True), _layer_spec(w_dw, l),
                  _layer_spec(w_down, w_down_l, single_buffer=True)] + c_in,
        out_specs=[tile] + c_out,
        scratch_shapes=[pltpu.VMEM((tm, d_ff), BF16)],
        compiler_params=_params("arbitrary", "arbitrary") if casts else _params("parallel", "parallel"),
        name="conv_ffn",
    )(h2, h2, h2, x, mod, g_post, w_up, w_dw, w_down, *c_args)
    return out if casts else out[0]


def _rope_tables(n_tokens):
    lane = np.arange(V7X_LANES)
    row_axis = ((lane % HEAD_DIM) // (2 * ROPE_FREQS)) == 0
    sign = np.where((lane % (2 * ROPE_FREQS)) < ROPE_FREQS, -1.0, 1.0).astype(np.float32)
    inv = ROPE_THETA ** (-jnp.arange(ROPE_FREQS, dtype=F32) / ROPE_FREQS)
    tok = jnp.arange(n_tokens)
    pos = jnp.where(jnp.asarray(row_axis)[None, :], (tok // GRID_W)[:, None], (tok % GRID_W)[:, None])
    ang = pos.astype(F32) * jnp.tile(inv, V7X_LANES // ROPE_FREQS)[None, :]
    return jnp.cos(ang), jnp.sin(ang) * sign[None, :]


def _kv_dup_weights(w_in):
    depth, d, _ = w_in.shape
    heads = w_in[:, :, ATTN_W:ATTN_W + 2 * KV_W].astype(BF16).reshape(depth, d, 2 * N_KV_HEADS, 1, HEAD_DIM)
    return jnp.broadcast_to(heads, (depth, d, 2 * N_KV_HEADS, 2, HEAD_DIM)).reshape(depth, d, 2 * KV_DUP_W)


def kernel(x, c, ctx, c_ctx, w_ada, b_ada, g_pre_mix, g_post_mix, g_pre_ffn, g_post_ffn, w_in, sink,
           w_conf_dw, b_conf_dw, conf_ln_g, conf_ln_b, w_sc_dw, g_group, w_out, w_up, w_ffn_dw, w_down):
    batch, seq, d = x.shape
    depth = w_in.shape[0]
    ctx_len = ctx.shape[1]
    assert batch < MOD_ROWS

    cc = jnp.concatenate([c, c_ctx[None, :], jnp.zeros((MOD_ROWS - batch - 1, d), F32)], axis=0)
    mod, w_in_first, w_out_first = _modulation(cc, w_ada, b_ada, ((w_in, 0), (w_out, 0)))
    mod = mod.reshape(depth * MOD_ROWS, 1, N_MOD * d)
    tables = _rope_tables(seq)
    sink_flat = sink.reshape(-1).astype(F32)
    w_kv = _kv_dup_weights(w_in)
    conv_params = (w_conf_dw, b_conf_dw, conf_ln_g, conf_ln_b, w_sc_dw, g_group)

    kv_outs = [(1, 0, KV_DUP_W, "k"), (1, KV_DUP_W, KV_DUP_W, "v_lo"), (1, KV_DUP_W, KV_DUP_W, "v_hi")]
    all_outs = [(0, 0, ATTN_W, "q")] + kv_outs

    w_in_b, w_out_b = (w_in_first, 0), (w_out_first, 0)

    for l in range(depth):
        update_ctx = l < depth - 1
        row = l * MOD_ROWS
        q, kk, vlo, vhi, zcs = _mixer_in(x, mod, row, True, g_pre_mix, w_in_b, (w_kv, l), l, tables, all_outs,
                                         conv_params, IN_TM)
        if update_ctx:
            qc, kc, vclo, vchi, zcs_c = _mixer_in(ctx, mod, row + batch, False, g_pre_mix, w_in_b, (w_kv, l), l,
                                                  None, all_outs, conv_params, ctx_len)
        else:
            kc, vclo, vchi = _mixer_in(ctx, mod, row + batch, False, g_pre_mix, w_in_b, (w_kv, l), l, None,
                                       kv_outs, None, ctx_len)
        x_mid, h2, w_up_l, w_down_l = _mix(sink_flat, l, q, kk, vlo, vhi, kc, vclo, vchi, zcs, x, mod, row, True,
                                           g_group, g_post_mix, g_pre_ffn, w_out_b, MIX_TQ,
                                           casts=((w_up, l), (w_down, l)))
        w_up_b, w_down_b = (w_up_l, 0), (w_down_l, 0)
        if update_ctx:
            ctx_mid, hc2 = _mix(sink_flat, l, qc, None, None, None, kc, vclo, vchi, zcs_c, ctx, mod,
                                row + batch, False, g_group, g_post_mix, g_pre_ffn, w_out_b, ctx_len)
            ctx = _ffn(hc2, ctx_mid, mod, row + batch, False, l, g_post_ffn, w_up_b, w_ffn_dw, w_down_b,
                       ctx_len)
        if l + 1 < depth:
            x, w_in_next, w_out_next = _ffn(h2, x_mid, mod, row, True, l, g_post_ffn, w_up_b, w_ffn_dw, w_down_b,
                                            FFN_TM, casts=((w_in, l + 1), (w_out, l + 1)))
            w_in_b, w_out_b = (w_in_next, 0), (w_out_next, 0)
        else:
            x = _ffn(h2, x_mid, mod, row, True, l, g_post_ffn, w_up_b, w_ffn_dw, w_down_b, FFN_TM)
    return x
```
